```python
import jax, jax.numpy as jnp
from jax import lax
import numpy as np


D_MODEL = 1024
BATCH = 16
SEQ = 2048
DEPTH = 4

N_MIXERS = 2
N_A = (DEPTH + 1) // 2
N_B = DEPTH // 2
EPS = 1e-6

A_HEADS = 4
A_DQK = D_MODEL // 8
A_DV = D_MODEL // 4
A_CHUNK = 64
A_TOK = 2 * A_HEADS * A_DQK + 2 * A_HEADS * A_DV + 2 * A_HEADS

B_HEADS = 16
B_KV_HEADS = 2
B_GROUP = B_HEADS // B_KV_HEADS
B_HD = D_MODEL // 16
B_WINDOW = 128
B_BLOCK = 128
B_TOK = (B_HEADS + 2 * B_KV_HEADS) * B_HD
ROT_DIM = B_HD // 4
ROPE_THETA = 500000.0

N_MEM = 256
M_HEADS = 4
M_HD = D_MODEL // 8
M_Q = M_HEADS * M_HD

A_IN = A_TOK + M_Q
B_IN = B_TOK + M_Q
A_OUT = A_HEADS * A_DV + M_Q
B_OUT = B_HEADS * B_HD + M_Q

D_FF = 256 * ((8 * D_MODEL // 3 + 255) // 256)

kernel_name = 'hybrid_mlstm_swa_sink_memxattn_macaron'


def rms_norm(x, g):
    xf = x.astype(jnp.float32)
    y = xf * lax.rsqrt(jnp.mean(xf * xf, axis=-1, keepdims=True) + EPS)
    return (y * g.astype(jnp.float32)).astype(x.dtype)


def swiglu(x, w_in, w_out):
    gate, up = jnp.split(x @ w_in, 2, axis=-1)
    return (jax.nn.silu(gate) * up) @ w_out


def rope_tables(positions):
    inv_freq = ROPE_THETA ** (-jnp.arange(0, ROT_DIM, 2, dtype=jnp.float32) / ROT_DIM)
    ang = positions.astype(jnp.float32)[..., None] * inv_freq
    return jnp.cos(ang), jnp.sin(ang)


def apply_partial_rope(x, cos, sin):
    half = ROT_DIM // 2
    x1, x2, rest = x[..., :half], x[..., half:ROT_DIM], x[..., ROT_DIM:]
    c, s = cos[:, :, None, :], sin[:, :, None, :]
    return jnp.concatenate([x1 * c - x2 * s, x2 * c + x1 * s, rest], axis=-1)


def mlstm_chunkwise(q, k, v, i_pre, logf):
    bsz, nh, t, dk = q.shape
    dv = v.shape[-1]
    nc = t // A_CHUNK

    def chunks(a):
        a = a.reshape(a.shape[:2] + (nc, A_CHUNK) + a.shape[3:])
        return jnp.moveaxis(a, 2, 0)

    causal = jnp.tril(jnp.ones((A_CHUNK, A_CHUNK), dtype=bool))

    def step(carry, inp):
        c_st, n_st, m_st = carry
        qc, kc, vc, ic, fc = inp
        b = jnp.cumsum(fc, axis=-1)
        log_d = b[..., :, None] - b[..., None, :] + ic[..., None, :]
        log_d = jnp.where(causal, log_d, -jnp.inf)
        m_inter = b + m_st[..., None]
        m_t = jnp.maximum(m_inter, jnp.max(log_d, axis=-1))
        s = jnp.einsum('bhjd,bhsd->bhjs', qc, kc) * jnp.exp(log_d - m_t[..., None])
        inter = jnp.exp(m_inter - m_t)
        num = (jnp.einsum('bhjs,bhsv->bhjv', s, vc)
               + inter[..., None] * jnp.einsum('bhjd,bhvd->bhjv', qc, c_st))
        den = jnp.sum(s, axis=-1) + inter * jnp.einsum('bhjd,bhd->bhj', qc, n_st)
        h = num / jnp.maximum(jnp.abs(den), jnp.exp(-m_t))[..., None]
        b_last = b[..., -1]
        log_w = b_last[..., None] - b + ic
        m_new = jnp.maximum(b_last + m_st, jnp.max(log_w, axis=-1))
        w = jnp.exp(log_w - m_new[..., None])
        decay = jnp.exp(b_last + m_st - m_new)
        c_new = decay[..., None, None] * c_st + jnp.einsum('bhsv,bhsd->bhvd', w[..., None] * vc, kc)
        n_new = decay[..., None] * n_st + jnp.einsum('bhs,bhsd->bhd', w, kc)
        return (c_new, n_new, m_new), h

    init = (jnp.zeros((bsz, nh, dv, dk), jnp.float32),
            jnp.zeros((bsz, nh, dk), jnp.float32),
            jnp.zeros((bsz, nh), jnp.float32))
    xs = (chunks(q * dk ** -0.5), chunks(k), chunks(v), chunks(i_pre), chunks(logf))
    _, hs = lax.scan(step, init, xs)
    return jnp.moveaxis(hs, 0, 2).reshape(bsz, nh, t, dv)


def mlstm_heads(tok, gate_b, h_norm_g):
    bsz, t, _ = tok.shape
    o1 = A_HEADS * A_DQK
    o2 = 2 * o1
    o3 = o2 + A_HEADS * A_DV
    o4 = o3 + A_HEADS * A_DV
    f32 = jnp.float32
    q = tok[..., :o1].reshape(bsz, t, A_HEADS, A_DQK).transpose(0, 2, 1, 3).astype(f32)
    k = tok[..., o1:o2].reshape(bsz, t, A_HEADS, A_DQK).transpose(0, 2, 1, 3).astype(f32)
    v = tok[..., o2:o3].reshape(bsz, t, A_HEADS, A_DV).transpose(0, 2, 1, 3).astype(f32)
    o_gate = jax.nn.sigmoid(tok[..., o3:o4].astype(f32))
    gates = tok[..., o4:].astype(f32) + gate_b.astype(f32)
    i_pre = gates[..., :A_HEADS].transpose(0, 2, 1)
    logf = jax.nn.log_sigmoid(gates[..., A_HEADS:]).transpose(0, 2, 1)
    h = mlstm_chunkwise(q, k, v, i_pre, logf).transpose(0, 2, 1, 3)
    h = rms_norm(h, h_norm_g.reshape(A_HEADS, A_DV)).reshape(bsz, t, A_HEADS * A_DV)
    return (h * o_gate).astype(tok.dtype)


def swa_heads(tok, cos, sin, q_norm_g, k_norm_g, sinks):
    bsz, t, _ = tok.shape
    f32 = jnp.float32
    nq = B_HEADS * B_HD
    nk = B_KV_HEADS * B_HD
    q = tok[..., :nq].reshape(bsz, t, B_HEADS, B_HD).astype(f32)
    k = tok[..., nq:nq + nk].reshape(bsz, t, B_KV_HEADS, B_HD).astype(f32)
    v = tok[..., nq + nk:].reshape(bsz, t, B_KV_HEADS, B_HD).astype(f32)
    q = apply_partial_rope(rms_norm(q, q_norm_g), cos, sin) * B_HD ** -0.5
    k = apply_partial_rope(rms_norm(k, k_norm_g), cos, sin)
    nb = t // B_BLOCK
    qb = q.reshape(bsz, nb, B_BLOCK, B_KV_HEADS, B_GROUP, B_HD)
    kb = k.reshape(bsz, nb, B_BLOCK, B_KV_HEADS, B_HD)
    vb = v.reshape(bsz, nb, B_BLOCK, B_KV_HEADS, B_HD)

    def band(a):
        prev = jnp.concatenate([jnp.zeros_like(a[:, :1]), a[:, :-1]], axis=1)
        return jnp.moveaxis(jnp.concatenate([prev, a], axis=2), 1, 0)

    qi = jnp.arange(B_BLOCK)[:, None]
    kj = jnp.arange(2 * B_BLOCK)[None, :]
    diff = B_BLOCK + qi - kj
    in_window = (diff >= 0) & (diff < B_WINDOW)
    sink = sinks.astype(f32).reshape(B_KV_HEADS, B_GROUP)[None, :, :, None, None]

    def block(args):
        n, qn, kn, vn = args
        valid = in_window & ((n * B_BLOCK - B_BLOCK + kj) >= 0)
        s = jnp.einsum('bqhgd,bkhd->bhgqk', qn, kn)
        s = jnp.where(valid, s, -jnp.inf)
        m = jnp.maximum(jnp.max(s, axis=-1, keepdims=True), sink)
        p = jnp.exp(s - m)
        denom = jnp.sum(p, axis=-1, keepdims=True) + jnp.exp(sink - m)
        return jnp.einsum('bhgqk,bkhd->bqhgd', p / denom, vn)

    out = lax.map(block, (jnp.arange(nb), jnp.moveaxis(qb, 1, 0), band(kb), band(vb)))
    return jnp.moveaxis(out, 0, 1).reshape(bsz, t, nq).astype(tok.dtype)


def memory_heads(xq, mem_k, mem_v, q_norm_g, k_norm_g):
    bsz, t, _ = xq.shape
    f32 = jnp.float32
    q = rms_norm(xq.reshape(bsz, t, M_HEADS, M_HD).astype(f32), q_norm_g)
    k = rms_norm(mem_k.astype(f32), k_norm_g)
    s = jnp.einsum('bthd,bmhd->bhtm', q, k) * M_HD ** -0.5
    p = jax.nn.softmax(s, axis=-1)
    o = jnp.einsum('bhtm,bmhd->bthd', p, mem_v.astype(f32))
    return o.reshape(bsz, t, M_Q).astype(xq.dtype)


def setup_inputs(seed: int = 0) -> dict:
    key = jax.random.key(seed)
    ks = jax.random.split(key, 32)
    f32 = jnp.float32

    def normal(k, shape, scale):
        return jax.random.normal(k, shape, f32) * scale

    def gain(k, shape):
        return 1.0 + 0.05 * jax.random.normal(k, shape, f32)

    x = normal(ks[0], (BATCH, SEQ, D_MODEL), 1.0)
    mem = normal(ks[1], (BATCH, N_MEM, D_MODEL), 1.0)
    start = jax.random.randint(ks[2], (BATCH, 1), 0, 4096, dtype=jnp.int32)
    positions = start + jnp.arange(SEQ, dtype=jnp.int32)[None, :]
    i_bias = normal(ks[3], (N_A, A_HEADS), 0.1)
    f_bias = jnp.linspace(3.0, 6.0, A_HEADS, dtype=f32)[None, :] + normal(ks[4], (N_A, A_HEADS), 0.1)
    return {
        'x': x,
        'mem': mem,
        'positions': positions,
        'mem_norm_g': gain(ks[5], (D_MODEL,)),
        'mem_w_kv': normal(ks[6], (D_MODEL, 2 * M_Q), D_MODEL ** -0.5),
        'ffn1_norm_g': gain(ks[7], (DEPTH, D_MODEL)),
        'ffn1_w_in': normal(ks[8], (DEPTH, D_MODEL, 2 * D_FF), D_MODEL ** -0.5),
        'ffn1_w_out': normal(ks[9], (DEPTH, D_FF, D_MODEL), 0.5 * D_FF ** -0.5),
        'mix_norm_g': gain(ks[10], (DEPTH, D_MODEL)),
        'ffn2_norm_g': gain(ks[11], (DEPTH, D_MODEL)),
        'ffn2_w_in': normal(ks[12], (DEPTH, D_MODEL, 2 * D_FF), D_MODEL ** -0.5),
        'ffn2_w_out': normal(ks[13], (DEPTH, D_FF, D_MODEL), 0.5 * D_FF ** -0.5),
        'xa_q_norm_g': gain(ks[14], (DEPTH, M_HD)),
        'xa_k_norm_g': gain(ks[15], (DEPTH, M_HD)),
        'a_w_in': normal(ks[16], (N_A, D_MODEL, A_IN), D_MODEL ** -0.5),
        'a_gate_b': jnp.concatenate([i_bias, f_bias], axis=-1),
        'a_h_norm_g': gain(ks[17], (N_A, A_HEADS * A_DV)),
        'a_w_out': normal(ks[18], (N_A, A_OUT, D_MODEL), 0.5 * A_OUT ** -0.5),
        'b_w_in': normal(ks[19], (N_B, D_MODEL, B_IN), D_MODEL ** -0.5),
        'b_q_norm_g': gain(ks[20], (N_B, B_HD)),
        'b_k_norm_g': gain(ks[21], (N_B, B_HD)),
        'b_sinks': normal(ks[22], (N_B, B_HEADS), 0.5),
        'b_w_out': normal(ks[23], (N_B, B_OUT, D_MODEL), 0.5 * B_OUT ** -0.5),
    }


def reference(x, mem, positions, mem_norm_g, mem_w_kv, ffn1_norm_g, ffn1_w_in, ffn1_w_out,
              mix_norm_g, ffn2_norm_g, ffn2_w_in, ffn2_w_out, xa_q_norm_g, xa_k_norm_g,
              a_w_in, a_gate_b, a_h_norm_g, a_w_out,
              b_w_in, b_q_norm_g, b_k_norm_g, b_sinks, b_w_out):
    bsz, n_mem, _ = mem.shape
    mem_kv = rms_norm(mem, mem_norm_g) @ mem_w_kv
    mem_k = mem_kv[..., :M_Q].reshape(bsz, n_mem, M_HEADS, M_HD)
    mem_v = mem_kv[..., M_Q:].reshape(bsz, n_mem, M_HEADS, M_HD)
    cos, sin = rope_tables(positions)
    for i in range(DEPTH):
        j = i // N_MIXERS
        x = x + 0.5 * swiglu(rms_norm(x, ffn1_norm_g[i]), ffn1_w_in[i], ffn1_w_out[i])
        hn = rms_norm(x, mix_norm_g[i])
        if i % N_MIXERS == 0:
            proj = hn @ a_w_in[j]
            y_tok = mlstm_heads(proj[..., :A_TOK], a_gate_b[j], a_h_norm_g[j])
            xq = proj[..., A_TOK:]
            w_out = a_w_out[j]
        else:
            proj = hn @ b_w_in[j]
            y_tok = swa_heads(proj[..., :B_TOK], cos, sin, b_q_norm_g[j], b_k_norm_g[j], b_sinks[j])
            xq = proj[..., B_TOK:]
            w_out = b_w_out[j]
        y_mem = memory_heads(xq, mem_k, mem_v, xa_q_norm_g[i], xa_k_norm_g[i])
        x = x + jnp.concatenate([y_tok, y_mem], axis=-1) @ w_out
        x = x + 0.5 * swiglu(rms_norm(x, ffn2_norm_g[i]), ffn2_w_in[i], ffn2_w_out[i])
    return x
```

```python
import functools

import numpy as np
import jax
import jax.numpy as jnp
from jax import lax
from jax.experimental import pallas as pl
from jax.experimental.pallas import tpu as pltpu

F32 = jnp.float32
BF16 = jnp.bfloat16

D_MODEL = 1024
DEPTH = 4
EPS = 1e-6
D_FF = 2816

A_HEADS = 4
A_DQK = 128
A_DV = 256
A_QK = A_HEADS * A_DQK
A_V = A_HEADS * A_DV
A_GATES = 2 * A_HEADS
A_TOK = 2 * A_QK + 2 * A_V + A_GATES

B_HEADS = 16
B_KV_HEADS = 2
B_GROUP = B_HEADS // B_KV_HEADS
B_HD = 64
B_WINDOW = 128
B_Q = B_HEADS * B_HD
B_KV = B_KV_HEADS * B_HD
B_TOK = B_Q + 2 * B_KV
ROT_DIM = 16
ROPE_THETA = 500000.0

N_MEM = 256
M_HEADS = 4
M_HD = 128
M_Q = M_HEADS * M_HD

LANES = 128
MXU_DIM = 256
VMEM_LIMIT = 56 * 1024 * 1024

ROW_TILE = 512
FF_TILE = MXU_DIM
MLSTM_CHUNK = 256
SWA_BLOCK = B_WINDOW

A_COL_Q, A_COL_K, A_COL_V, A_COL_O = 0, A_QK, 2 * A_QK, 2 * A_QK + A_V
A_COL_XQ = 2 * A_QK + 2 * A_V
A_MAIN = A_COL_XQ + M_Q
B_COL_XQ = B_Q
B_COL_KV = B_Q + M_Q
B_MAIN = B_COL_KV + 2 * B_KV


def _params(*semantics):
    return pltpu.CompilerParams(dimension_semantics=semantics, vmem_limit_bytes=VMEM_LIMIT)


def _resident(block_shape, index_map):
    return pl.BlockSpec(block_shape, index_map, pipeline_mode=pl.Buffered(1))


def _rms(x, gain):
    ms = jnp.mean(x * x, axis=-1, keepdims=True)
    return x * lax.rsqrt(ms + EPS) * gain


def _dot(a, b):
    return jnp.dot(a, b, preferred_element_type=F32)


def _dot_nt(a, b):
    return lax.dot_general(a, b, (((1,), (1,)), ((), ())), preferred_element_type=F32)


def _split_dot(x, w):
    hi = x.astype(BF16)
    lo = (x - hi.astype(F32)).astype(BF16)
    return _dot(hi, w) + _dot(lo, w)


def _ffn_kernel(x_ref, g_ref, win_ref, wout_ref, o_ref):
    x = x_ref[...]
    xn = _rms(x, g_ref[...]).astype(BF16)
    acc = jnp.zeros_like(x)
    for j in range(D_FF // FF_TILE):
        lo = j * FF_TILE
        gate = _dot(xn, win_ref[:, lo:lo + FF_TILE])
        up = _dot(xn, win_ref[:, D_FF + lo:D_FF + lo + FF_TILE])
        h = (gate * jax.nn.sigmoid(gate) * up).astype(BF16)
        acc = acc + _dot(h, wout_ref[lo:lo + FF_TILE, :])
    o_ref[...] = x + 0.5 * acc


def _ffn(x, gains, w_in, w_out, layer):
    n = x.shape[0]
    return pl.pallas_call(
        _ffn_kernel,
        out_shape=jax.ShapeDtypeStruct((n, D_MODEL), F32),
        grid=(n // ROW_TILE,),
        in_specs=[
            pl.BlockSpec((ROW_TILE, D_MODEL), lambda i: (i, 0)),
            _resident((None, 1, D_MODEL), lambda i: (layer, 0, 0)),
            _resident((None, D_MODEL, 2 * D_FF), lambda i: (layer, 0, 0)),
            _resident((None, D_FF, D_MODEL), lambda i: (layer, 0, 0)),
        ],
        out_specs=pl.BlockSpec((ROW_TILE, D_MODEL), lambda i: (i, 0)),
        compiler_params=_params("parallel"),
        name="ffn",
    )(x, gains, w_in, w_out)


def _memkv_kernel(mem_ref, g_ref, w_ref, k_ref, v_ref):
    mn = _rms(mem_ref[...], g_ref[...]).astype(BF16)
    kv = _dot(mn, w_ref[...])
    for h in range(M_HEADS):
        kh = kv[:, h * M_HD:(h + 1) * M_HD]
        ms = jnp.mean(kh * kh, axis=-1, keepdims=True)
        k_ref[:, h * M_HD:(h + 1) * M_HD] = (kh * lax.rsqrt(ms + EPS)).astype(BF16)
    v_ref[...] = kv[:, M_Q:].astype(BF16)


def _memkv(mem, gain, w_kv):
    bsz = mem.shape[0]
    return pl.pallas_call(
        _memkv_kernel,
        out_shape=(jax.ShapeDtypeStruct((bsz, N_MEM, M_Q), BF16),
                   jax.ShapeDtypeStruct((bsz, N_MEM, M_Q), BF16)),
        grid=(bsz,),
        in_specs=[
            pl.BlockSpec((None, N_MEM, D_MODEL), lambda b: (b, 0, 0)),
            _resident((1, D_MODEL), lambda b: (0, 0)),
            _resident((D_MODEL, 2 * M_Q), lambda b: (0, 0)),
        ],
        out_specs=(pl.BlockSpec((None, N_MEM, M_Q), lambda b: (b, 0, 0)),
                   pl.BlockSpec((None, N_MEM, M_Q), lambda b: (b, 0, 0))),
        compiler_params=_params("parallel"),
        name="memkv",
    )(mem, gain, w_kv)


def _store_mem_queries(xq, gq_ref, gk_ref, out_ref, col0):
    geff = gq_ref[...] * gk_ref[...] * (M_HD ** -0.5)
    for h in range(M_HEADS):
        s = xq[:, h * M_HD:(h + 1) * M_HD]
        ms = jnp.mean(s * s, axis=-1, keepdims=True)
        out_ref[:, col0 + h * M_HD:col0 + (h + 1) * M_HD] = (s * lax.rsqrt(ms + EPS) * geff).astype(BF16)


def _inproj_a_kernel(x_ref, g_ref, w_ref, wg_ref, gb_ref, gq_ref, gk_ref, main_ref, gates_ref):
    xn = _rms(x_ref[...], g_ref[...]).astype(BF16)
    step = 2 * MXU_DIM
    for c in range(A_COL_XQ // step):
        main_ref[:, c * step:(c + 1) * step] = _dot(xn, w_ref[:, c * step:(c + 1) * step]).astype(BF16)
    xq = _dot(xn, w_ref[:, A_COL_XQ:A_MAIN])
    _store_mem_queries(xq, gq_ref, gk_ref, main_ref, A_COL_XQ)
    gates_ref[...] = _dot(xn, wg_ref[...]) + gb_ref[...]


def _inproj_a(x, mix_g, w_main, w_gates, gate_b, gq, gk, layer, j):
    n = x.shape[0]
    return pl.pallas_call(
        _inproj_a_kernel,
        out_shape=(jax.ShapeDtypeStruct((n, A_MAIN), BF16),
                   jax.ShapeDtypeStruct((n, LANES), F32)),
        grid=(n // ROW_TILE,),
        in_specs=[
            pl.BlockSpec((ROW_TILE, D_MODEL), lambda i: (i, 0)),
            _resident((None, 1, D_MODEL), lambda i: (layer, 0, 0)),
            _resident((None, D_MODEL, A_MAIN), lambda i: (j, 0, 0)),
            _resident((None, D_MODEL, LANES), lambda i: (j, 0, 0)),
            _resident((None, 1, LANES), lambda i: (j, 0, 0)),
            _resident((None, 1, M_HD), lambda i: (layer, 0, 0)),
            _resident((None, 1, M_HD), lambda i: (layer, 0, 0)),
        ],
        out_specs=(pl.BlockSpec((ROW_TILE, A_MAIN), lambda i: (i, 0)),
                   pl.BlockSpec((ROW_TILE, LANES), lambda i: (i, 0))),
        compiler_params=_params("parallel"),
        name="inproj_mlstm",
    )(x, mix_g, w_main, w_gates, gate_b, gq, gk)


def _rope(x, cos, sin_up, sin_down):
    return (x * cos + pltpu.roll(x, LANES - ROT_DIM // 2, axis=1) * sin_up
            + pltpu.roll(x, ROT_DIM // 2, axis=1) * sin_down)


def _inproj_b_kernel(x_ref, pos_ref, g_ref, w_ref, gqk_ref, tab_ref, gq_ref, gk_ref, main_ref):
    xn = _rms(x_ref[...], g_ref[...]).astype(BF16)
    ang = pos_ref[...].astype(F32) * tab_ref[0:1, :]
    cos = jnp.cos(ang)
    sin = jnp.sin(ang)
    sin_up = sin * tab_ref[1:2, :]
    sin_down = sin * tab_ref[2:3, :]
    r = lax.broadcasted_iota(jnp.int32, (MXU_DIM, MXU_DIM), 0) // B_HD
    c = lax.broadcasted_iota(jnp.int32, (MXU_DIM, MXU_DIM), 1) // B_HD
    seg = (r == c).astype(BF16)

    def norm_rope(slab, gain, scale, col0):
        ssum = _split_dot(slab * slab, seg)
        nrm = slab * lax.rsqrt(ssum * (1.0 / B_HD) + EPS)
        for half in range(MXU_DIM // LANES):
            xh = nrm[:, half * LANES:(half + 1) * LANES] * gain
            out = _rope(xh, cos, sin_up, sin_down)
            if scale != 1.0:
                out = out * scale
            main_ref[:, col0 + half * LANES:col0 + (half + 1) * LANES] = out.astype(BF16)

    gq_lane = gqk_ref[0:1, :]
    gk_lane = gqk_ref[1:2, :]
    for s in range(B_Q // MXU_DIM):
        slab = _dot(xn, w_ref[:, s * MXU_DIM:(s + 1) * MXU_DIM])
        norm_rope(slab, gq_lane, B_HD ** -0.5, s * MXU_DIM)
    xq = _dot(xn, w_ref[:, B_COL_XQ:B_COL_KV])
    _store_mem_queries(xq, gq_ref, gk_ref, main_ref, B_COL_XQ)
    kv = _dot(xn, w_ref[:, B_COL_KV:B_MAIN])
    v = kv[:, B_KV:]
    norm_rope(kv, gk_lane, 1.0, B_COL_KV)
    main_ref[:, B_COL_KV + B_KV:B_MAIN] = v.astype(BF16)


def _inproj_b(x, pos, mix_g, w_main, gqk, rope_tab, gq, gk, layer, j):
    n = x.shape[0]
    return pl.pallas_call(
        _inproj_b_kernel,
        out_shape=jax.ShapeDtypeStruct((n, B_MAIN), BF16),
        grid=(n // ROW_TILE,),
        in_specs=[
            pl.BlockSpec((ROW_TILE, D_MODEL), lambda i: (i, 0)),
            pl.BlockSpec((ROW_TILE, 1), lambda i: (i, 0)),
            _resident((None, 1, D_MODEL), lambda i: (layer, 0, 0)),
            _resident((None, D_MODEL, B_MAIN), lambda i: (j, 0, 0)),
            _resident((None, 2, LANES), lambda i: (j, 0, 0)),
            _resident((3, LANES), lambda i: (0, 0)),
            _resident((None, 1, M_HD), lambda i: (layer, 0, 0)),
            _resident((None, 1, M_HD), lambda i: (layer, 0, 0)),
        ],
        out_specs=pl.BlockSpec((ROW_TILE, B_MAIN), lambda i: (i, 0)),
        compiler_params=_params("parallel"),
        name="inproj_swa",
    )(x, pos, mix_g, w_main, gqk, rope_tab, gq, gk)


A_STATE_W = A_DV + LANES


def _log_sigmoid(x):
    return jnp.minimum(x, 0.0) - jnp.log(1.0 + jnp.exp(-jnp.abs(x)))


def _mlstm_kernel(q_ref, k_ref, v_ref, o_ref, g_ref, hg_ref, y_ref, ct_sc, m_sc):
    chunk = MLSTM_CHUNK

    @pl.when(pl.program_id(1) == 0)
    def _():
        ct_sc[...] = jnp.zeros_like(ct_sc)
        m_sc[...] = jnp.zeros_like(m_sc)

    g = g_ref[...]
    logf = _log_sigmoid(g)
    row = lax.broadcasted_iota(jnp.int32, (chunk, chunk), 0)
    col = lax.broadcasted_iota(jnp.int32, (chunk, chunk), 1)
    causal = col <= row
    tri = causal.astype(BF16)
    hi = logf.astype(BF16)
    rem = logf - hi.astype(F32)
    mid = rem.astype(BF16)
    low = (rem - mid.astype(F32)).astype(BF16)
    b = _dot(tri, hi) + _dot(tri, mid) + _dot(tri, low)
    b = pltpu.roll(b, LANES - A_HEADS, axis=1)
    a = g - b
    a_t = a.T
    ones_col = (lax.broadcasted_iota(jnp.int32, (chunk, LANES), 1) == 0).astype(BF16)
    scale = A_DQK ** -0.5

    for h in range(A_HEADS):
        m_st = m_sc[h:h + 1, :][:, 0:1]
        am = jnp.where(causal, a_t[h:h + 1, :], -jnp.inf)
        big_m = jnp.maximum(jnp.max(am, axis=1, keepdims=True), m_st)
        decay_mat = jnp.exp(am - big_m)
        qh = q_ref[:, h * A_DQK:(h + 1) * A_DQK]
        kh = k_ref[:, h * A_DQK:(h + 1) * A_DQK]
        p = (_dot_nt(qh, kh) * scale * decay_mat).astype(BF16)
        v_ext = jnp.concatenate([v_ref[:, h * A_DV:(h + 1) * A_DV], ones_col], axis=1)
        state = ct_sc[h]
        inter = scale * jnp.exp(m_st - big_m)
        tot = _dot(p, v_ext) + inter * _dot(qh, state.astype(BF16))
        num = tot[:, :A_DV]
        den = tot[:, A_DV:A_DV + 1]
        b_h = b[:, h:h + 1]
        inv = 1.0 / jnp.maximum(jnp.abs(den), jnp.exp(-(b_h + big_m)))
        hn = _rms(num * inv, hg_ref[:, h * A_DV:(h + 1) * A_DV])
        og = jax.nn.sigmoid(o_ref[:, h * A_DV:(h + 1) * A_DV].astype(F32))
        y_ref[:, h * A_DV:(h + 1) * A_DV] = (hn * og).astype(BF16)
        m_last = big_m[chunk - 1:chunk, :]
        w = jnp.exp(a[:, h:h + 1] - m_last)
        decay = jnp.exp(m_st - m_last)
        vw = (v_ext.astype(F32) * w).astype(BF16)
        k_t = kh.astype(F32).T.astype(BF16)
        ct_sc[h] = decay * state + _dot(k_t, vw)
        m_sc[h:h + 1, :] = jnp.broadcast_to(b_h[chunk - 1:chunk, :] + m_last, (1, LANES))


def _mlstm(main, gates, h_gain, j, bsz, seq):
    nc = seq // MLSTM_CHUNK
    return pl.pallas_call(
        _mlstm_kernel,
        out_shape=jax.ShapeDtypeStruct((bsz * seq, A_V), BF16),
        grid=(bsz, nc),
        in_specs=[
            pl.BlockSpec((MLSTM_CHUNK, A_QK), lambda b, c: (b * nc + c, A_COL_Q // A_QK)),
            pl.BlockSpec((MLSTM_CHUNK, A_QK), lambda b, c: (b * nc + c, A_COL_K // A_QK)),
            pl.BlockSpec((MLSTM_CHUNK, A_V), lambda b, c: (b * nc + c, A_COL_V // A_V)),
            pl.BlockSpec((MLSTM_CHUNK, A_V), lambda b, c: (b * nc + c, A_COL_O // A_V)),
            pl.BlockSpec((MLSTM_CHUNK, LANES), lambda b, c: (b * nc + c, 0)),
            _resident((None, 1, A_V), lambda b, c: (j, 0, 0)),
        ],
        out_specs=pl.BlockSpec((MLSTM_CHUNK, A_V), lambda b, c: (b * nc + c, 0)),
        scratch_shapes=[pltpu.VMEM((A_HEADS, A_DQK, A_STATE_W), F32),
                        pltpu.VMEM((8, LANES), F32)],
        compiler_params=_params("parallel", "arbitrary"),
        name="mlstm",
    )(main, main, main, main, gates, h_gain)


def _swa_kernel(sink_ref, q_ref, kvp_ref, kvc_ref, y_ref):
    blk = SWA_BLOCK
    n = pl.program_id(1)
    kv = jnp.concatenate([kvp_ref[...], kvc_ref[...]], axis=0).astype(F32)
    keys = kv[:, :B_KV]
    vals = kv[:, B_KV:]
    lane_kv = lax.broadcasted_iota(jnp.int32, (2 * blk, LANES), 1)
    lane_q = lax.broadcasted_iota(jnp.int32, (blk, LANES), 1)
    qi = lax.broadcasted_iota(jnp.int32, (blk, 2 * blk), 0)
    kj = lax.broadcasted_iota(jnp.int32, (blk, 2 * blk), 1)
    dist = blk + qi - kj
    valid = (dist >= 0) & (dist < B_WINDOW) & ((kj >= blk) | (n > 0))

    for grp in range(B_KV_HEADS):
        own = (lane_kv // B_HD) == grp
        kg = jnp.where(own, keys, 0.0)
        kg = (kg + pltpu.roll(kg, B_HD, axis=1)).astype(BF16)
        vg = jnp.where(own, vals, 0.0)
        vg = (vg + pltpu.roll(vg, B_HD, axis=1)).astype(BF16)
        slabs = []
        for pp in range(B_GROUP // 2):
            slab = q_ref[:, (grp * (B_GROUP // 2) + pp) * LANES:(grp * (B_GROUP // 2) + pp + 1) * LANES]
            zero = jnp.zeros_like(slab)
            slabs.append(jnp.where(lane_q < B_HD, slab, zero))
            slabs.append(jnp.where(lane_q >= B_HD, slab, zero))
        scores = _dot_nt(jnp.concatenate(slabs, axis=0), kg)
        probs = []
        inv = []
        for hh in range(B_GROUP):
            s = jnp.where(valid, scores[hh * blk:(hh + 1) * blk], -jnp.inf)
            sink = sink_ref[grp * B_GROUP + hh]
            m = jnp.maximum(jnp.max(s, axis=1, keepdims=True), sink)
            p = jnp.exp(s - m)
            inv.append(1.0 / (jnp.sum(p, axis=1, keepdims=True) + jnp.exp(sink - m)))
            probs.append(p.astype(BF16))
        out = _dot(jnp.concatenate(probs, axis=0), vg)
        for pp in range(B_GROUP // 2):
            even = out[(2 * pp) * blk:(2 * pp + 1) * blk] * inv[2 * pp]
            odd = out[(2 * pp + 1) * blk:(2 * pp + 2) * blk] * inv[2 * pp + 1]
            c0 = (grp * (B_GROUP // 2) + pp) * LANES
            y_ref[:, c0:c0 + LANES] = jnp.where(lane_q < B_HD, even, odd).astype(BF16)


def _swa(main, sinks, bsz, seq):
    nb = seq // SWA_BLOCK
    kv_w = 2 * B_KV
    kv_col = B_COL_KV // kv_w
    return pl.pallas_call(
        _swa_kernel,
        out_shape=jax.ShapeDtypeStruct((bsz * seq, B_Q), BF16),
        grid=(bsz, nb),
        in_specs=[
            pl.BlockSpec(memory_space=pltpu.SMEM),
            pl.BlockSpec((SWA_BLOCK, B_Q), lambda b, n: (b * nb + n, 0)),
            pl.BlockSpec((SWA_BLOCK, kv_w), lambda b, n: (b * nb + jnp.maximum(n - 1, 0), kv_col)),
            pl.BlockSpec((SWA_BLOCK, kv_w), lambda b, n: (b * nb + n, kv_col)),
        ],
        out_specs=pl.BlockSpec((SWA_BLOCK, B_Q), lambda b, n: (b * nb + n, 0)),
        compiler_params=_params("parallel", "arbitrary"),
        name="swa",
    )(sinks, main, main, main)


def _outproj_kernel(x_ref, yt_ref, xq_ref, mk_ref, mv_ref, w_ref, o_ref):
    y_dim = yt_ref.shape[1]
    acc = x_ref[...] + _dot(yt_ref[...], w_ref[:y_dim, :])
    heads = []
    for h in range(M_HEADS):
        sl = slice(h * M_HD, (h + 1) * M_HD)
        s = _dot_nt(xq_ref[:, sl], mk_ref[:, sl])
        p = jnp.exp(s - jnp.max(s, axis=1, keepdims=True))
        inv = 1.0 / jnp.sum(p, axis=1, keepdims=True)
        heads.append((_dot(p.astype(BF16), mv_ref[:, sl]) * inv).astype(BF16))
    o_ref[...] = acc + _dot(jnp.concatenate(heads, axis=1), w_ref[y_dim:, :])


def _outproj(x, y_tok, main, xq_col, mem_k, mem_v, w_out, j, bsz, seq):
    nt = seq // ROW_TILE
    y_dim = y_tok.shape[1]
    return pl.pallas_call(
        _outproj_kernel,
        out_shape=jax.ShapeDtypeStruct((bsz * seq, D_MODEL), F32),
        grid=(bsz, nt),
        in_specs=[
            pl.BlockSpec((ROW_TILE, D_MODEL), lambda b, t: (b * nt + t, 0)),
            pl.BlockSpec((ROW_TILE, y_dim), lambda b, t: (b * nt + t, 0)),
            pl.BlockSpec((ROW_TILE, M_Q), lambda b, t: (b * nt + t, xq_col // M_Q)),
            pl.BlockSpec((None, N_MEM, M_Q), lambda b, t: (b, 0, 0)),
            pl.BlockSpec((None, N_MEM, M_Q), lambda b, t: (b, 0, 0)),
            _resident((None, y_dim + M_Q, D_MODEL), lambda b, t: (j, 0, 0)),
        ],
        out_specs=pl.BlockSpec((ROW_TILE, D_MODEL), lambda b, t: (b * nt + t, 0)),
        compiler_params=_params("parallel", "parallel"),
        name="outproj",
    )(x, y_tok, main, mem_k, mem_v, w_out)


def _rope_lane_tables():
    half = ROT_DIM // 2
    inv_freq = ROPE_THETA ** (-np.arange(0, ROT_DIM, 2, dtype=np.float32) / ROT_DIM)
    d = np.arange(LANES) % B_HD
    tab = np.zeros((3, LANES), np.float32)
    tab[0] = np.where(d < ROT_DIM, inv_freq[d % half], 0.0)
    tab[1] = np.where(d < half, -1.0, 0.0)
    tab[2] = np.where((d >= half) & (d < ROT_DIM), 1.0, 0.0)
    return jnp.asarray(tab)


def kernel(x, mem, positions, mem_norm_g, mem_w_kv, ffn1_norm_g, ffn1_w_in, ffn1_w_out,
           mix_norm_g, ffn2_norm_g, ffn2_w_in, ffn2_w_out, xa_q_norm_g, xa_k_norm_g,
           a_w_in, a_gate_b, a_h_norm_g, a_w_out,
           b_w_in, b_q_norm_g, b_k_norm_g, b_sinks, b_w_out):
    bsz, seq, _ = x.shape
    n = bsz * seq
    assert seq % ROW_TILE == 0 and seq % MLSTM_CHUNK == 0 and seq % SWA_BLOCK == 0

    bf = lambda w: w.astype(BF16)
    row3 = lambda g: g.reshape(g.shape[0], 1, g.shape[1])
    ffn1_in, ffn1_out, ffn2_in, ffn2_out = bf(ffn1_w_in), bf(ffn1_w_out), bf(ffn2_w_in), bf(ffn2_w_out)
    a_main = bf(jnp.concatenate([a_w_in[..., :A_COL_XQ], a_w_in[..., A_TOK:]], axis=-1))
    a_gw = bf(jnp.pad(a_w_in[..., A_COL_XQ:A_TOK], ((0, 0), (0, 0), (0, LANES - A_GATES))))
    a_gb = row3(jnp.pad(a_gate_b, ((0, 0), (0, LANES - A_GATES))))
    b_main = bf(jnp.concatenate([b_w_in[..., :B_Q], b_w_in[..., B_TOK:], b_w_in[..., B_Q:B_TOK]], axis=-1))
    b_gqk = jnp.stack([jnp.tile(b_q_norm_g, (1, LANES // B_HD)),
                       jnp.tile(b_k_norm_g, (1, LANES // B_HD))], axis=1)
    a_out, b_out = bf(a_w_out), bf(b_w_out)
    rope_tab = _rope_lane_tables()
    pos = positions.reshape(n, 1)
    g_ffn1, g_mix, g_ffn2 = row3(ffn1_norm_g), row3(mix_norm_g), row3(ffn2_norm_g)
    g_xq, g_xk, g_h = row3(xa_q_norm_g), row3(xa_k_norm_g), row3(a_h_norm_g)

    mem_k, mem_v = _memkv(mem, mem_norm_g.reshape(1, D_MODEL), bf(mem_w_kv))
    xf = x.reshape(n, D_MODEL)
    for i in range(DEPTH):
        j = i // 2
        xf = _ffn(xf, g_ffn1, ffn1_in, ffn1_out, i)
        if i % 2 == 0:
            main, gates = _inproj_a(xf, g_mix, a_main, a_gw, a_gb, g_xq, g_xk, i, j)
            y_tok = _mlstm(main, gates, g_h, j, bsz, seq)
            xf = _outproj(xf, y_tok, main, A_COL_XQ, mem_k, mem_v, a_out, j, bsz, seq)
        else:
            main = _inproj_b(xf, pos, g_mix, b_main, b_gqk, rope_tab, g_xq, g_xk, i, j)
            y_tok = _swa(main, b_sinks[j], bsz, seq)
            xf = _outproj(xf, y_tok, main, B_COL_XQ, mem_k, mem_v, b_out, j, bsz, seq)
        xf = _ffn(xf, g_ffn2, ffn2_in, ffn2_out, i)
    return xf.reshape(bsz, seq, D_MODEL)
```

```python
import math

import numpy as np
import jax
import jax.numpy as jnp
from jax import lax
from jax.experimental import pallas as pl
from jax.experimental.pallas import tpu as pltpu

F32 = jnp.float32
BF16 = jnp.bfloat16

D_MODEL = 1024
DEPTH = 4
EPS = 1e-6
D_FF = 2816

A_HEADS = 4
A_DQK = 128
A_DV = 256
A_QK = A_HEADS * A_DQK
A_V = A_HEADS * A_DV
A_GATES = 2 * A_HEADS
A_TOK = 2 * A_QK + 2 * A_V + A_GATES

B_HEADS = 16
B_KV_HEADS = 2
B_GROUP = B_HEADS // B_KV_HEADS
B_HD = 64
B_WINDOW = 128
B_Q = B_HEADS * B_HD
B_KV = B_KV_HEADS * B_HD
B_TOK = B_Q + 2 * B_KV
ROT_DIM = 16
ROT_HALF = ROT_DIM // 2
ROPE_THETA = 500000.0

N_MEM = 256
M_HEADS = 4
M_HD = 128
M_Q = M_HEADS * M_HD

LOG2E = math.log2(math.e)

LANES = 128
SUBLANES = 8
MXU_DIM = 256
VMEM_LIMIT = 56 * 1024 * 1024

ROW_TILE = 512
FF_TILE = MXU_DIM
MLSTM_CHUNK = 256
SWA_BLOCK = B_WINDOW
SWA_STEP = 4 * SWA_BLOCK

A_COL_Q, A_COL_K, A_COL_V, A_COL_O = 0, A_QK, 2 * A_QK, 2 * A_QK + A_V
A_COL_XQ = 2 * A_QK + 2 * A_V
A_MAIN = A_COL_XQ + M_Q
G_LANE_A, G_LANE_B, G_LANE_CM = 0, A_HEADS, 2 * A_HEADS
B_COL_XQ = B_Q
B_COL_KV = B_Q + M_Q
B_MAIN = B_COL_KV + 2 * B_KV


def _params(*semantics):
    return pltpu.CompilerParams(dimension_semantics=semantics, vmem_limit_bytes=VMEM_LIMIT)


def _resident(block_shape, index_map):
    return pl.BlockSpec(block_shape, index_map, pipeline_mode=pl.Buffered(1))


def _rms(x, gain):
    ms = jnp.mean(x * x, axis=-1, keepdims=True)
    return x * lax.rsqrt(ms + EPS) * gain


def _dot(a, b):
    return jnp.dot(a, b, preferred_element_type=F32)


def _dot_nt(a, b):
    return lax.dot_general(a, b, (((1,), (1,)), ((), ())), preferred_element_type=F32)


def _split2(x):
    hi = x.astype(BF16)
    return hi, (x - hi.astype(F32)).astype(BF16)


def _ffn_kernel(x_ref, g_ref, win_ref, wout_ref, o_ref):
    x = x_ref[...]
    xn = _rms(x, g_ref[...]).astype(BF16)
    acc = jnp.zeros_like(x)
    for j in range(D_FF // FF_TILE):
        lo = j * FF_TILE
        gate = _dot(xn, win_ref[:, lo:lo + FF_TILE])
        up = _dot(xn, win_ref[:, D_FF + lo:D_FF + lo + FF_TILE])
        h = (gate * jax.nn.sigmoid(gate) * up).astype(BF16)
        acc = acc + _dot(h, wout_ref[lo:lo + FF_TILE, :])
    o_ref[...] = x + 0.5 * acc


def _ffn(x, gains, w_in, w_out, layer):
    n = x.shape[0]
    return pl.pallas_call(
        _ffn_kernel,
        out_shape=jax.ShapeDtypeStruct((n, D_MODEL), F32),
        grid=(n // ROW_TILE,),
        in_specs=[
            pl.BlockSpec((ROW_TILE, D_MODEL), lambda i: (i, 0)),
            _resident((None, 1, D_MODEL), lambda i: (layer, 0, 0)),
            _resident((None, D_MODEL, 2 * D_FF), lambda i: (layer, 0, 0)),
            _resident((None, D_FF, D_MODEL), lambda i: (layer, 0, 0)),
        ],
        out_specs=pl.BlockSpec((ROW_TILE, D_MODEL), lambda i: (i, 0)),
        compiler_params=_params("parallel"),
        name="ffn",
    )(x, gains, w_in, w_out)


def _memkv_kernel(mem_ref, g_ref, w_ref, k_ref, v_ref):
    mn = _rms(mem_ref[...], g_ref[...]).astype(BF16)
    kv = _dot(mn, w_ref[...])
    for h in range(M_HEADS):
        kh = kv[:, h * M_HD:(h + 1) * M_HD]
        ms = jnp.mean(kh * kh, axis=-1, keepdims=True)
        k_ref[:, h * M_HD:(h + 1) * M_HD] = (kh * lax.rsqrt(ms + EPS)).astype(BF16)
    v_ref[...] = kv[:, M_Q:].astype(BF16)


def _memkv(mem, gain, w_kv):
    bsz = mem.shape[0]
    return pl.pallas_call(
        _memkv_kernel,
        out_shape=(jax.ShapeDtypeStruct((bsz, N_MEM, M_Q), BF16),
                   jax.ShapeDtypeStruct((bsz, N_MEM, M_Q), BF16)),
        grid=(bsz,),
        in_specs=[
            pl.BlockSpec((None, N_MEM, D_MODEL), lambda b: (b, 0, 0)),
            _resident((1, D_MODEL), lambda b: (0, 0)),
            _resident((D_MODEL, 2 * M_Q), lambda b: (0, 0)),
        ],
        out_specs=(pl.BlockSpec((None, N_MEM, M_Q), lambda b: (b, 0, 0)),
                   pl.BlockSpec((None, N_MEM, M_Q), lambda b: (b, 0, 0))),
        compiler_params=_params("parallel"),
        name="memkv",
    )(mem, gain, w_kv)


def _store_mem_queries(xq, gq_ref, gk_ref, out_ref, col0):
    geff = gq_ref[...] * gk_ref[...] * (M_HD ** -0.5)
    for h in range(M_HEADS):
        s = xq[:, h * M_HD:(h + 1) * M_HD]
        ms = jnp.mean(s * s, axis=-1, keepdims=True)
        out_ref[:, col0 + h * M_HD:col0 + (h + 1) * M_HD] = (s * lax.rsqrt(ms + EPS) * geff).astype(BF16)


def _log_sigmoid(x):
    return jnp.minimum(x, 0.0) - jnp.log(1.0 + jnp.exp(-jnp.abs(x)))


def _causal_mask(n):
    row = lax.broadcasted_iota(jnp.int32, (n, n), 0)
    col = lax.broadcasted_iota(jnp.int32, (n, n), 1)
    return col <= row


def _mlstm_gate_record(g):
    chunk = g.shape[0]
    causal = _causal_mask(chunk)
    hi, lo = _split2(_log_sigmoid(g))
    both = _dot(causal.astype(BF16), jnp.concatenate([hi, lo], axis=1))
    b = both[:, :LANES] + both[:, LANES:]
    a = g - pltpu.roll(b, LANES - A_HEADS, axis=1)
    a_t = a.T
    lane = lax.broadcasted_iota(jnp.int32, (chunk, LANES), 1)
    rec = jnp.where(lane < G_LANE_B, a, jnp.where(lane < G_LANE_CM, b, 0.0))
    for h in range(A_HEADS):
        run_max = jnp.max(jnp.where(causal, a_t[h:h + 1, :], -jnp.inf), axis=1, keepdims=True)
        rec = jnp.where(lane == G_LANE_CM + h, run_max, rec)
    return rec


def _inproj_a_kernel(x_ref, g_ref, w_ref, wg_ref, gb_ref, gq_ref, gk_ref, main_ref, gates_ref):
    xn = _rms(x_ref[...], g_ref[...]).astype(BF16)
    step = 2 * MXU_DIM

    def main_chunk(c):
        y = _dot(xn, w_ref[:, c * step:(c + 1) * step])
        if c * step >= A_COL_O:
            y = jax.nn.sigmoid(y)
        main_ref[:, c * step:(c + 1) * step] = y.astype(BF16)

    gates = _dot(xn, wg_ref[...]) + gb_ref[...]
    main_chunk(0)
    main_chunk(1)
    for c in range(ROW_TILE // MLSTM_CHUNK):
        rows = slice(c * MLSTM_CHUNK, (c + 1) * MLSTM_CHUNK)
        gates_ref[rows, :] = _mlstm_gate_record(gates[rows])
    for c in range(2, A_COL_XQ // step):
        main_chunk(c)
    xq = _dot(xn, w_ref[:, A_COL_XQ:A_MAIN])
    _store_mem_queries(xq, gq_ref, gk_ref, main_ref, A_COL_XQ)


def _inproj_a(x, mix_g, w_main, w_gates, gate_b, gq, gk, layer, j):
    n = x.shape[0]
    return pl.pallas_call(
        _inproj_a_kernel,
        out_shape=(jax.ShapeDtypeStruct((n, A_MAIN), BF16),
                   jax.ShapeDtypeStruct((n, LANES), F32)),
        grid=(n // ROW_TILE,),
        in_specs=[
            pl.BlockSpec((ROW_TILE, D_MODEL), lambda i: (i, 0)),
            _resident((None, 1, D_MODEL), lambda i: (layer, 0, 0)),
            _resident((None, D_MODEL, A_MAIN), lambda i: (j, 0, 0)),
            _resident((None, D_MODEL, LANES), lambda i: (j, 0, 0)),
            _resident((None, 1, LANES), lambda i: (j, 0, 0)),
            _resident((None, 1, M_HD), lambda i: (layer, 0, 0)),
            _resident((None, 1, M_HD), lambda i: (layer, 0, 0)),
        ],
        out_specs=(pl.BlockSpec((ROW_TILE, A_MAIN), lambda i: (i, 0)),
                   pl.BlockSpec((ROW_TILE, LANES), lambda i: (i, 0))),
        compiler_params=_params("parallel"),
        name="inproj_mlstm",
    )(x, mix_g, w_main, w_gates, gate_b, gq, gk)


ROPE_PAD = B_HD - ROT_DIM


def _inproj_b_kernel(x_ref, pos_ref, g_ref, w_ref, gqk_ref, tab_ref, gq_ref, gk_ref, main_ref):
    xn = _rms(x_ref[...], g_ref[...]).astype(BF16)
    ang = tab_ref[:, 0:1] * pos_ref[...].astype(F32)
    cos_c = jnp.cos(ang)
    sin_c = jnp.sin(ang) * tab_ref[:, LANES:LANES + 1]
    pad = jnp.zeros((ROPE_PAD, ang.shape[1]), F32)
    cos_t = jnp.concatenate([cos_c, pad, cos_c, pad], axis=0).T
    sin_t = jnp.concatenate([sin_c, pad, sin_c, pad], axis=0).T
    lane = lax.broadcasted_iota(jnp.int32, (1, LANES), 1) % B_HD
    cos_t = cos_t + (lane >= ROT_DIM).astype(F32)
    first_half = lane < ROT_HALF
    r = lax.broadcasted_iota(jnp.int32, (MXU_DIM, MXU_DIM), 0) // B_HD
    c = lax.broadcasted_iota(jnp.int32, (MXU_DIM, MXU_DIM), 1) // B_HD
    seg = jnp.where(r == c, 1.0 / B_HD, 0.0).astype(BF16)

    def norm_rope(slab, cos_g, sin_g, col0):
        hi, lo = _split2(slab * slab)
        inv = lax.rsqrt(_dot(hi, seg) + _dot(lo, seg) + EPS)
        for half in range(MXU_DIM // LANES):
            sl = slice(half * LANES, (half + 1) * LANES)
            xh = slab[:, sl]
            partner = jnp.where(first_half, pltpu.roll(xh, LANES - ROT_HALF, axis=1),
                                pltpu.roll(xh, ROT_HALF, axis=1))
            out = (xh * cos_g + partner * sin_g) * inv[:, sl]
            main_ref[:, col0 + half * LANES:col0 + (half + 1) * LANES] = out.astype(BF16)

    q_scale = B_HD ** -0.5 * LOG2E
    cos_q, sin_q = cos_t * (gqk_ref[0:1, :] * q_scale), sin_t * (gqk_ref[1:2, :] * q_scale)
    cos_k, sin_k = cos_t * gqk_ref[2:3, :], sin_t * gqk_ref[3:4, :]
    n_q = B_Q // MXU_DIM
    slab = _dot(xn, w_ref[:, 0:MXU_DIM])
    for s in range(n_q):
        nxt = (s + 1) * MXU_DIM if s + 1 < n_q else B_COL_KV
        ahead = _dot(xn, w_ref[:, nxt:nxt + MXU_DIM])
        norm_rope(slab, cos_q, sin_q, s * MXU_DIM)
        slab = ahead
    xq = _dot(xn, w_ref[:, B_COL_XQ:B_COL_KV])
    v = slab[:, B_KV:]
    norm_rope(slab, cos_k, sin_k, B_COL_KV)
    main_ref[:, B_COL_KV + B_KV:B_MAIN] = v.astype(BF16)
    _store_mem_queries(xq, gq_ref, gk_ref, main_ref, B_COL_XQ)


def _inproj_b(x, pos, mix_g, w_main, gqk, rope_tab, gq, gk, layer, j):
    n = x.shape[0]
    return pl.pallas_call(
        _inproj_b_kernel,
        out_shape=jax.ShapeDtypeStruct((n, B_MAIN), BF16),
        grid=(n // ROW_TILE,),
        in_specs=[
            pl.BlockSpec((ROW_TILE, D_MODEL), lambda i: (i, 0)),
            pl.BlockSpec((None, 1, ROW_TILE), lambda i: (i, 0, 0)),
            _resident((None, 1, D_MODEL), lambda i: (layer, 0, 0)),
            _resident((None, D_MODEL, B_MAIN), lambda i: (j, 0, 0)),
            _resident((None, 4, LANES), lambda i: (j, 0, 0)),
            _resident((ROT_DIM, 2 * LANES), lambda i: (0, 0)),
            _resident((None, 1, M_HD), lambda i: (layer, 0, 0)),
            _resident((None, 1, M_HD), lambda i: (layer, 0, 0)),
        ],
        out_specs=pl.BlockSpec((ROW_TILE, B_MAIN), lambda i: (i, 0)),
        compiler_params=_params("parallel"),
        name="inproj_swa",
    )(x, pos, mix_g, w_main, gqk, rope_tab, gq, gk)


A_STATE_W = A_DV + LANES


def _mlstm_kernel(q_ref, k_ref, v_ref, og_ref, g_ref, hg_ref, y_ref, ct_sc, m_sc):
    chunk = MLSTM_CHUNK
    scale = A_DQK ** -0.5

    @pl.when(pl.program_id(1) == 0)
    def _():
        ct_sc[...] = jnp.zeros_like(ct_sc)
        m_sc[...] = jnp.zeros_like(m_sc)

    rec = g_ref[...]
    a_t = rec.T
    b = pltpu.roll(rec, LANES - G_LANE_B, axis=1)
    run_max = pltpu.roll(rec, LANES - G_LANE_CM, axis=1)
    m_row = m_sc[0:1, :]
    big_m = jnp.maximum(run_max, m_row)
    inter = scale * jnp.exp(m_row - big_m)
    floor = jnp.exp(-(b + big_m))
    m_last = big_m[chunk - 1:chunk, :]
    decay = jnp.exp(m_row - m_last)
    m_sc[0:1, :] = b[chunk - 1:chunk, :] + m_last
    k_t = k_ref[...].astype(F32).T
    causal = _causal_mask(chunk)
    ones_col = (lax.broadcasted_iota(jnp.int32, (chunk, LANES), 1) == 0).astype(BF16)

    heads = range(A_HEADS)
    qk_cols = [slice(h * A_DQK, (h + 1) * A_DQK) for h in heads]
    v_cols = [slice(h * A_DV, (h + 1) * A_DV) for h in heads]
    for h in heads:
        a_row = a_t[h:h + 1, :]
        decay_mat = jnp.exp(jnp.where(causal, (a_row + math.log(scale)) - big_m[:, h:h + 1], -jnp.inf))
        qh = q_ref[:, qk_cols[h]]
        p = (_dot_nt(qh, k_ref[:, qk_cols[h]]) * decay_mat).astype(BF16)
        q_inter = (qh.astype(F32) * inter[:, h:h + 1]).astype(BF16)
        v_ext = jnp.concatenate([v_ref[:, v_cols[h]], ones_col], axis=1)
        state = ct_sc[h]
        tot = _dot(jnp.concatenate([p, q_inter], axis=1),
                   jnp.concatenate([v_ext, state.astype(BF16)], axis=0))
        num = tot[:, :A_DV]
        den = tot[:, A_DV:A_DV + 1]
        inv = 1.0 / jnp.maximum(jnp.abs(den), floor[:, h:h + 1])
        ms = jnp.mean(num * num, axis=1, keepdims=True)
        fac = inv * lax.rsqrt(inv * inv * ms + EPS)
        y_ref[:, v_cols[h]] = (num * fac * hg_ref[:, v_cols[h]] * og_ref[:, v_cols[h]].astype(F32)).astype(BF16)
        w_row = jnp.exp(a_row - m_last[:, h:h + 1])
        kw_t = (k_t[qk_cols[h], :] * w_row).astype(BF16)
        ct_sc[h] = decay[:, h:h + 1] * state + _dot(kw_t, v_ext)


def _mlstm(main, gates, h_gain, j, bsz, seq):
    nc = seq // MLSTM_CHUNK
    return pl.pallas_call(
        _mlstm_kernel,
        out_shape=jax.ShapeDtypeStruct((bsz * seq, A_V), BF16),
        grid=(bsz, nc),
        in_specs=[
            pl.BlockSpec((MLSTM_CHUNK, A_QK), lambda b, c: (b * nc + c, A_COL_Q // A_QK)),
            pl.BlockSpec((MLSTM_CHUNK, A_QK), lambda b, c: (b * nc + c, A_COL_K // A_QK)),
            pl.BlockSpec((MLSTM_CHUNK, A_V), lambda b, c: (b * nc + c, A_COL_V // A_V)),
            pl.BlockSpec((MLSTM_CHUNK, A_V), lambda b, c: (b * nc + c, A_COL_O // A_V)),
            pl.BlockSpec((MLSTM_CHUNK, LANES), lambda b, c: (b * nc + c, 0)),
            _resident((None, 1, A_V), lambda b, c: (j, 0, 0)),
        ],
        out_specs=pl.BlockSpec((MLSTM_CHUNK, A_V), lambda b, c: (b * nc + c, 0)),
        scratch_shapes=[pltpu.VMEM((A_HEADS, A_DQK, A_STATE_W), F32),
                        pltpu.VMEM((SUBLANES, LANES), F32)],
        compiler_params=_params("parallel", "arbitrary"),
        name="mlstm",
    )(main, main, main, main, gates, h_gain)


def _swa_kernel(sink_ref, q_ref, kvp_ref, kvc_ref, y_ref):
    blk = SWA_BLOCK
    kv = jnp.concatenate([kvp_ref[...], kvc_ref[...]], axis=0).astype(F32)
    keys = kv[:, :B_KV]
    vals = kv[:, B_KV:]
    lane_kv = lax.broadcasted_iota(jnp.int32, keys.shape, 1)
    lane_q = lax.broadcasted_iota(jnp.int32, (blk, LANES), 1)
    qi = lax.broadcasted_iota(jnp.int32, (blk, 2 * blk), 0)
    kj = lax.broadcasted_iota(jnp.int32, (blk, 2 * blk), 1)
    dist = blk + qi - kj
    in_window = (dist >= 0) & (dist < B_WINDOW)
    valid_first = in_window & ((kj >= blk) | (pl.program_id(1) > 0))

    def both_halves(x, grp):
        own = jnp.where((lane_kv // B_HD) == grp, x, 0.0)
        return (own + pltpu.roll(own, B_HD, axis=1)).astype(BF16)

    kg = [both_halves(keys, grp) for grp in range(B_KV_HEADS)]
    vg = [both_halves(vals, grp) for grp in range(B_KV_HEADS)]
    sinks = [sink_ref[h] * LOG2E for h in range(B_HEADS)]
    units = [(grp, sub) for grp in range(B_KV_HEADS) for sub in range(SWA_STEP // blk)]

    def scores_of(unit):
        grp, sub = unit
        slabs = []
        for pp in range(B_GROUP // 2):
            c0 = (grp * (B_GROUP // 2) + pp) * LANES
            slab = q_ref[sub * blk:(sub + 1) * blk, c0:c0 + LANES]
            zero = jnp.zeros_like(slab)
            slabs.append(jnp.where(lane_q < B_HD, slab, zero))
            slabs.append(jnp.where(lane_q >= B_HD, slab, zero))
        return _dot_nt(jnp.concatenate(slabs, axis=0), kg[grp][sub * blk:(sub + 2) * blk])

    for grp, sub in units:
        scores = scores_of((grp, sub))
        valid = valid_first if sub == 0 else in_window
        probs = []
        inv = []
        for hh in range(B_GROUP):
            sink = sinks[grp * B_GROUP + hh]
            s = jnp.where(valid, scores[hh * blk:(hh + 1) * blk], -jnp.inf)
            m = jnp.maximum(jnp.max(s, axis=1, keepdims=True), sink)
            p = jnp.exp2(s - m)
            inv.append(1.0 / (jnp.sum(p, axis=1, keepdims=True) + jnp.exp2(sink - m)))
            probs.append(p.astype(BF16))
        out = _dot(jnp.concatenate(probs, axis=0), vg[grp][sub * blk:(sub + 2) * blk])
        for pp in range(B_GROUP // 2):
            even = out[(2 * pp) * blk:(2 * pp + 1) * blk] * inv[2 * pp]
            odd = out[(2 * pp + 1) * blk:(2 * pp + 2) * blk] * inv[2 * pp + 1]
            c0 = (grp * (B_GROUP // 2) + pp) * LANES
            y_ref[sub * blk:(sub + 1) * blk, c0:c0 + LANES] = jnp.where(lane_q < B_HD, even, odd).astype(BF16)


def _swa(main, sinks, bsz, seq):
    ns = seq // SWA_STEP
    per_step = SWA_STEP // SWA_BLOCK
    kv_w = 2 * B_KV
    kv_col = B_COL_KV // kv_w
    return pl.pallas_call(
        _swa_kernel,
        out_shape=jax.ShapeDtypeStruct((bsz * seq, B_Q), BF16),
        grid=(bsz, ns),
        in_specs=[
            pl.BlockSpec(memory_space=pltpu.SMEM),
            pl.BlockSpec((SWA_STEP, B_Q), lambda b, n: (b * ns + n, 0)),
            pl.BlockSpec((SWA_BLOCK, kv_w),
                         lambda b, n: ((b * ns + n) * per_step - jnp.minimum(n, 1), kv_col)),
            pl.BlockSpec((SWA_STEP, kv_w), lambda b, n: (b * ns + n, kv_col)),
        ],
        out_specs=pl.BlockSpec((SWA_STEP, B_Q), lambda b, n: (b * ns + n, 0)),
        compiler_params=_params("parallel", "arbitrary"),
        name="swa",
    )(sinks, main, main, main)


def _outproj_kernel(x_ref, yt_ref, xq_ref, mk_ref, mv_ref, w_ref, o_ref):
    y_dim = yt_ref.shape[1]
    head_cols = [slice(h * M_HD, (h + 1) * M_HD) for h in range(M_HEADS)]
    scores = [_dot_nt(xq_ref[:, sl], mk_ref[:, sl]) for sl in head_cols]
    acc = x_ref[...] + _dot(yt_ref[...], w_ref[:y_dim, :])
    heads = []
    for s, sl in zip(scores, head_cols):
        p = jnp.exp(s - jnp.max(s, axis=1, keepdims=True))
        inv = 1.0 / jnp.sum(p, axis=1, keepdims=True)
        heads.append((_dot(p.astype(BF16), mv_ref[:, sl]) * inv).astype(BF16))
    o_ref[...] = acc + _dot(jnp.concatenate(heads, axis=1), w_ref[y_dim:, :])


def _outproj(x, y_tok, main, xq_col, mem_k, mem_v, w_out, j, bsz, seq):
    nt = seq // ROW_TILE
    y_dim = y_tok.shape[1]
    return pl.pallas_call(
        _outproj_kernel,
        out_shape=jax.ShapeDtypeStruct((bsz * seq, D_MODEL), F32),
        grid=(bsz, nt),
        in_specs=[
            pl.BlockSpec((ROW_TILE, D_MODEL), lambda b, t: (b * nt + t, 0)),
            pl.BlockSpec((ROW_TILE, y_dim), lambda b, t: (b * nt + t, 0)),
            pl.BlockSpec((ROW_TILE, M_Q), lambda b, t: (b * nt + t, xq_col // M_Q)),
            pl.BlockSpec((None, N_MEM, M_Q), lambda b, t: (b, 0, 0)),
            pl.BlockSpec((None, N_MEM, M_Q), lambda b, t: (b, 0, 0)),
            _resident((None, y_dim + M_Q, D_MODEL), lambda b, t: (j, 0, 0)),
        ],
        out_specs=pl.BlockSpec((ROW_TILE, D_MODEL), lambda b, t: (b * nt + t, 0)),
        compiler_params=_params("parallel", "parallel"),
        name="outproj",
    )(x, y_tok, main, mem_k, mem_v, w_out)


def _rope_row_tables():
    inv_freq = ROPE_THETA ** (-np.arange(0, ROT_DIM, 2, dtype=np.float32) / ROT_DIM)
    tab = np.zeros((ROT_DIM, 2 * LANES), np.float32)
    tab[:, :LANES] = np.concatenate([inv_freq, inv_freq])[:, None]
    tab[:, LANES:] = np.where(np.arange(ROT_DIM) < ROT_HALF, -1.0, 1.0)[:, None]
    return jnp.asarray(tab)


def _rope_gain_rows(gain):
    d = np.arange(B_HD)
    partner = np.where(d < ROT_HALF, d + ROT_HALF, np.where(d < ROT_DIM, d - ROT_HALF, d))
    reps = LANES // B_HD
    return jnp.stack([jnp.tile(gain, (1, reps)), jnp.tile(gain[:, partner], (1, reps))], axis=1)


def kernel(x, mem, positions, mem_norm_g, mem_w_kv, ffn1_norm_g, ffn1_w_in, ffn1_w_out,
           mix_norm_g, ffn2_norm_g, ffn2_w_in, ffn2_w_out, xa_q_norm_g, xa_k_norm_g,
           a_w_in, a_gate_b, a_h_norm_g, a_w_out,
           b_w_in, b_q_norm_g, b_k_norm_g, b_sinks, b_w_out):
    bsz, seq, _ = x.shape
    n = bsz * seq
    assert seq % ROW_TILE == 0 and seq % SWA_STEP == 0 and ROW_TILE % MLSTM_CHUNK == 0

    bf = lambda w: w.astype(BF16)
    row3 = lambda g: g.reshape(g.shape[0], 1, g.shape[1])
    ffn1_in, ffn1_out, ffn2_in, ffn2_out = bf(ffn1_w_in), bf(ffn1_w_out), bf(ffn2_w_in), bf(ffn2_w_out)
    a_main = bf(jnp.concatenate([a_w_in[..., :A_COL_XQ], a_w_in[..., A_TOK:]], axis=-1))
    a_gw = bf(jnp.pad(a_w_in[..., A_COL_XQ:A_TOK], ((0, 0), (0, 0), (0, LANES - A_GATES))))
    a_gb = row3(jnp.pad(a_gate_b, ((0, 0), (0, LANES - A_GATES))))
    b_main = bf(jnp.concatenate([b_w_in[..., :B_Q], b_w_in[..., B_TOK:], b_w_in[..., B_Q:B_TOK]], axis=-1))
    b_gqk = jnp.concatenate([_rope_gain_rows(b_q_norm_g), _rope_gain_rows(b_k_norm_g)], axis=1)
    a_out, b_out = bf(a_w_out), bf(b_w_out)
    rope_tab = _rope_row_tables()
    pos = positions.reshape(n // ROW_TILE, 1, ROW_TILE)
    g_ffn1, g_mix, g_ffn2 = row3(ffn1_norm_g), row3(mix_norm_g), row3(ffn2_norm_g)
    g_xq, g_xk, g_h = row3(xa_q_norm_g), row3(xa_k_norm_g), row3(a_h_norm_g)

    mem_k, mem_v = _memkv(mem, mem_norm_g.reshape(1, D_MODEL), bf(mem_w_kv))
    xf = x.reshape(n, D_MODEL)
    for i in range(DEPTH):
        j = i // 2
        xf = _ffn(xf, g_ffn1, ffn1_in, ffn1_out, i)
        if i % 2 == 0:
            main, gates = _inproj_a(xf, g_mix, a_main, a_gw, a_gb, g_xq, g_xk, i, j)
            y_tok = _mlstm(main, gates, g_h, j, bsz, seq)
            xf = _outproj(xf, y_tok, main, A_COL_XQ, mem_k, mem_v, a_out, j, bsz, seq)
        else:
            main = _inproj_b(xf, pos, g_mix, b_main, b_gqk, rope_tab, g_xq, g_xk, i, j)
            y_tok = _swa(main, b_sinks[j], bsz, seq)
            xf = _outproj(xf, y_tok, main, B_COL_XQ, mem_k, mem_v, b_out, j, bsz, seq)
        xf = _ffn(xf, g_ffn2, ffn2_in, ffn2_out, i)
    return xf.reshape(bsz, seq, D_MODEL)
```

```python
import math

import numpy as np
import jax
import jax.numpy as jnp
from jax import lax
from jax.experimental import pallas as pl
from jax.experimental.pallas import tpu as pltpu

F32 = jnp.float32
BF16 = jnp.bfloat16

D_MODEL = 1024
DEPTH = 4
EPS = 1e-6
D_FF = 2816

A_HEADS = 4
A_DQK = 128
A_DV = 256
A_QK = A_HEADS * A_DQK
A_V = A_HEADS * A_DV
A_GATES = 2 * A_HEADS
A_TOK = 2 * A_QK + 2 * A_V + A_GATES

B_HEADS = 16
B_KV_HEADS = 2
B_GROUP = B_HEADS // B_KV_HEADS
B_HD = 64
B_WINDOW = 128
B_Q = B_HEADS * B_HD
B_KV = B_KV_HEADS * B_HD
B_TOK = B_Q + 2 * B_KV
ROT_DIM = 16
ROT_HALF = ROT_DIM // 2
ROPE_THETA = 500000.0

N_MEM = 256
M_HEADS = 4
M_HD = 128
M_Q = M_HEADS * M_HD

LOG2E = math.log2(math.e)

LANES = 128
SUBLANES = 8
MXU_DIM = 256
VMEM_LIMIT = 56 * 1024 * 1024

ROW_TILE = 512
FF_TILE = MXU_DIM
FFN_ROWS = 1024
FFN_PART = 512
MLSTM_CHUNK = 256
SWA_STEP = 4 * B_WINDOW

A_COL_Q, A_COL_K, A_COL_V, A_COL_O = 0, A_QK, 2 * A_QK, 2 * A_QK + A_V
A_COL_XQ = 2 * A_QK + 2 * A_V
A_MAIN = A_COL_XQ + M_Q
G_LANE_A, G_LANE_B, G_LANE_CM = 0, A_HEADS, 2 * A_HEADS
B_COL_XQ = B_Q
B_COL_KV = B_Q + M_Q
B_MAIN = B_COL_KV + 2 * B_KV


def _params(*semantics):
    return pltpu.CompilerParams(dimension_semantics=semantics, vmem_limit_bytes=VMEM_LIMIT)


def _resident(block_shape, index_map):
    return pl.BlockSpec(block_shape, index_map, pipeline_mode=pl.Buffered(1))


def _rms(x, gain):
    ms = jnp.mean(x * x, axis=-1, keepdims=True)
    return x * lax.rsqrt(ms + EPS) * gain


def _dot(a, b):
    return jnp.dot(a, b, preferred_element_type=F32)


def _dot_nt(a, b):
    return lax.dot_general(a, b, (((1,), (1,)), ((), ())), preferred_element_type=F32)


def _split2(x):
    hi = x.astype(BF16)
    return hi, (x - hi.astype(F32)).astype(BF16)


def _ffn_kernel(x_ref, g_ref, win_ref, wout_ref, o_ref):
    parts = [slice(r * FFN_PART, (r + 1) * FFN_PART) for r in range(FFN_ROWS // FFN_PART)]
    x = [x_ref[rows, :] for rows in parts]
    xn = [_rms(xr, g_ref[...]).astype(BF16) for xr in x]
    acc = [jnp.zeros_like(xr) for xr in x]
    for j in range(D_FF // FF_TILE):
        lo = j * FF_TILE
        for r in range(len(parts)):
            gate = _dot(xn[r], win_ref[:, lo:lo + FF_TILE])
            up = _dot(xn[r], win_ref[:, D_FF + lo:D_FF + lo + FF_TILE])
            h = (gate * jax.nn.sigmoid(gate) * up).astype(BF16)
            acc[r] = acc[r] + _dot(h, wout_ref[lo:lo + FF_TILE, :])
    for r, rows in enumerate(parts):
        o_ref[rows, :] = x[r] + 0.5 * acc[r]


def _ffn(x, gains, w_in, w_out, layer):
    n = x.shape[0]
    return pl.pallas_call(
        _ffn_kernel,
        out_shape=jax.ShapeDtypeStruct((n, D_MODEL), F32),
        grid=(n // FFN_ROWS,),
        in_specs=[
            pl.BlockSpec((FFN_ROWS, D_MODEL), lambda i: (i, 0)),
            _resident((None, 1, D_MODEL), lambda i: (layer, 0, 0)),
            _resident((None, D_MODEL, 2 * D_FF), lambda i: (layer, 0, 0)),
            _resident((None, D_FF, D_MODEL), lambda i: (layer, 0, 0)),
        ],
        out_specs=pl.BlockSpec((FFN_ROWS, D_MODEL), lambda i: (i, 0)),
        compiler_params=_params("parallel"),
        name="ffn",
    )(x, gains, w_in, w_out)


def _memkv_kernel(mem_ref, g_ref, w_ref, k_ref, v_ref):
    mn = _rms(mem_ref[...], g_ref[...]).astype(BF16)
    kv = _dot(mn, w_ref[...])
    for h in range(M_HEADS):
        kh = kv[:, h * M_HD:(h + 1) * M_HD]
        ms = jnp.mean(kh * kh, axis=-1, keepdims=True)
        k_ref[:, h * M_HD:(h + 1) * M_HD] = (kh * lax.rsqrt(ms + EPS)).astype(BF16)
    v_ref[...] = kv[:, M_Q:].T.astype(BF16)


def _memkv(mem, gain, w_kv):
    bsz = mem.shape[0]
    return pl.pallas_call(
        _memkv_kernel,
        out_shape=(jax.ShapeDtypeStruct((bsz, N_MEM, M_Q), BF16),
                   jax.ShapeDtypeStruct((bsz, M_Q, N_MEM), BF16)),
        grid=(bsz,),
        in_specs=[
            pl.BlockSpec((None, N_MEM, D_MODEL), lambda b: (b, 0, 0)),
            _resident((1, D_MODEL), lambda b: (0, 0)),
            _resident((D_MODEL, 2 * M_Q), lambda b: (0, 0)),
        ],
        out_specs=(pl.BlockSpec((None, N_MEM, M_Q), lambda b: (b, 0, 0)),
                   pl.BlockSpec((None, M_Q, N_MEM), lambda b: (b, 0, 0))),
        compiler_params=_params("parallel"),
        name="memkv",
    )(mem, gain, w_kv)


def _store_mem_queries(xq, gq_ref, gk_ref, out_ref, col0):
    geff = gq_ref[...] * gk_ref[...] * (M_HD ** -0.5)
    for h in range(M_HEADS):
        s = xq[:, h * M_HD:(h + 1) * M_HD]
        ms = jnp.mean(s * s, axis=-1, keepdims=True)
        out_ref[:, col0 + h * M_HD:col0 + (h + 1) * M_HD] = (s * lax.rsqrt(ms + EPS) * geff).astype(BF16)


def _log_sigmoid(x):
    return jnp.minimum(x, 0.0) - jnp.log(1.0 + jnp.exp(-jnp.abs(x)))


def _causal_mask(n):
    row = lax.broadcasted_iota(jnp.int32, (n, n), 0)
    col = lax.broadcasted_iota(jnp.int32, (n, n), 1)
    return col <= row


def _mlstm_gate_record(g):
    chunk = g.shape[0]
    causal = _causal_mask(chunk)
    hi, lo = _split2(_log_sigmoid(g))
    both = _dot(causal.astype(BF16), jnp.concatenate([hi, lo], axis=1))
    b = both[:, :LANES] + both[:, LANES:]
    a = g - pltpu.roll(b, LANES - A_HEADS, axis=1)
    a_t = a.T
    lane = lax.broadcasted_iota(jnp.int32, (chunk, LANES), 1)
    rec = jnp.where(lane < G_LANE_B, a, jnp.where(lane < G_LANE_CM, b, 0.0))
    for h in range(A_HEADS):
        run_max = jnp.max(jnp.where(causal, a_t[h:h + 1, :], -jnp.inf), axis=1, keepdims=True)
        rec = jnp.where(lane == G_LANE_CM + h, run_max, rec)
    return rec


def _inproj_a_kernel(x_ref, g_ref, w_ref, wg_ref, gb_ref, gq_ref, gk_ref, main_ref, gates_ref):
    xn = _rms(x_ref[...], g_ref[...]).astype(BF16)
    step = 2 * MXU_DIM

    def main_chunk(c):
        y = _dot(xn, w_ref[:, c * step:(c + 1) * step])
        if c * step >= A_COL_O:
            y = jax.nn.sigmoid(y)
        main_ref[:, c * step:(c + 1) * step] = y.astype(BF16)

    gates = _dot(xn, wg_ref[...]) + gb_ref[...]
    main_chunk(0)
    main_chunk(1)
    for c in range(ROW_TILE // MLSTM_CHUNK):
        rows = slice(c * MLSTM_CHUNK, (c + 1) * MLSTM_CHUNK)
        gates_ref[rows, :] = _mlstm_gate_record(gates[rows])
    for c in range(2, A_COL_XQ // step):
        main_chunk(c)
    xq = _dot(xn, w_ref[:, A_COL_XQ:A_MAIN])
    _store_mem_queries(xq, gq_ref, gk_ref, main_ref, A_COL_XQ)


def _inproj_a(x, mix_g, w_main, w_gates, gate_b, gq, gk, layer, j):
    n = x.shape[0]
    return pl.pallas_call(
        _inproj_a_kernel,
        out_shape=(jax.ShapeDtypeStruct((n, A_MAIN), BF16),
                   jax.ShapeDtypeStruct((n, LANES), F32)),
        grid=(n // ROW_TILE,),
        in_specs=[
            pl.BlockSpec((ROW_TILE, D_MODEL), lambda i: (i, 0)),
            _resident((None, 1, D_MODEL), lambda i: (layer, 0, 0)),
            _resident((None, D_MODEL, A_MAIN), lambda i: (j, 0, 0)),
            _resident((None, D_MODEL, LANES), lambda i: (j, 0, 0)),
            _resident((None, 1, LANES), lambda i: (j, 0, 0)),
            _resident((None, 1, M_HD), lambda i: (layer, 0, 0)),
            _resident((None, 1, M_HD), lambda i: (layer, 0, 0)),
        ],
        out_specs=(pl.BlockSpec((ROW_TILE, A_MAIN), lambda i: (i, 0)),
                   pl.BlockSpec((ROW_TILE, LANES), lambda i: (i, 0))),
        compiler_params=_params("parallel"),
        name="inproj_mlstm",
    )(x, mix_g, w_main, w_gates, gate_b, gq, gk)


ROPE_PAD = B_HD - ROT_DIM


def _inproj_b_kernel(x_ref, pos_ref, g_ref, w_ref, gqk_ref, tab_ref, gq_ref, gk_ref, main_ref):
    xn = _rms(x_ref[...], g_ref[...]).astype(BF16)
    ang = tab_ref[:, 0:1] * pos_ref[...].astype(F32)
    cos_c = jnp.cos(ang)
    sin_c = jnp.sin(ang) * tab_ref[:, LANES:LANES + 1]
    pad = jnp.zeros((ROPE_PAD, ang.shape[1]), F32)
    cos_t = jnp.concatenate([cos_c, pad, cos_c, pad], axis=0).T
    sin_t = jnp.concatenate([sin_c, pad, sin_c, pad], axis=0).T
    lane = lax.broadcasted_iota(jnp.int32, (1, LANES), 1) % B_HD
    cos_t = cos_t + (lane >= ROT_DIM).astype(F32)
    first_half = lane < ROT_HALF
    r = lax.broadcasted_iota(jnp.int32, (MXU_DIM, MXU_DIM), 0) // B_HD
    c = lax.broadcasted_iota(jnp.int32, (MXU_DIM, MXU_DIM), 1) // B_HD
    seg = jnp.where(r == c, 1.0 / B_HD, 0.0).astype(BF16)

    def norm_rope(slab, cos_g, sin_g, col0):
        hi, lo = _split2(slab * slab)
        inv = lax.rsqrt(_dot(hi, seg) + _dot(lo, seg) + EPS)
        for half in range(MXU_DIM // LANES):
            sl = slice(half * LANES, (half + 1) * LANES)
            xh = slab[:, sl]
            partner = jnp.where(first_half, pltpu.roll(xh, LANES - ROT_HALF, axis=1),
                                pltpu.roll(xh, ROT_HALF, axis=1))
            out = (xh * cos_g + partner * sin_g) * inv[:, sl]
            main_ref[:, col0 + half * LANES:col0 + (half + 1) * LANES] = out.astype(BF16)

    q_scale = B_HD ** -0.5 * LOG2E
    cos_q, sin_q = cos_t * (gqk_ref[0:1, :] * q_scale), sin_t * (gqk_ref[1:2, :] * q_scale)
    cos_k, sin_k = cos_t * gqk_ref[2:3, :], sin_t * gqk_ref[3:4, :]
    n_q = B_Q // MXU_DIM
    slab = _dot(xn, w_ref[:, 0:MXU_DIM])
    for s in range(n_q):
        nxt = (s + 1) * MXU_DIM if s + 1 < n_q else B_COL_KV
        ahead = _dot(xn, w_ref[:, nxt:nxt + MXU_DIM])
        norm_rope(slab, cos_q, sin_q, s * MXU_DIM)
        slab = ahead
    xq = _dot(xn, w_ref[:, B_COL_XQ:B_COL_KV])
    v = slab[:, B_KV:]
    norm_rope(slab, cos_k, sin_k, B_COL_KV)
    main_ref[:, B_COL_KV + B_KV:B_MAIN] = v.astype(BF16)
    _store_mem_queries(xq, gq_ref, gk_ref, main_ref, B_COL_XQ)


def _inproj_b(x, pos, mix_g, w_main, gqk, rope_tab, gq, gk, layer, j):
    n = x.shape[0]
    return pl.pallas_call(
        _inproj_b_kernel,
        out_shape=jax.ShapeDtypeStruct((n, B_MAIN), BF16),
        grid=(n // ROW_TILE,),
        in_specs=[
            pl.BlockSpec((ROW_TILE, D_MODEL), lambda i: (i, 0)),
            pl.BlockSpec((None, 1, ROW_TILE), lambda i: (i, 0, 0)),
            _resident((None, 1, D_MODEL), lambda i: (layer, 0, 0)),
            _resident((None, D_MODEL, B_MAIN), lambda i: (j, 0, 0)),
            _resident((None, 4, LANES), lambda i: (j, 0, 0)),
            _resident((ROT_DIM, 2 * LANES), lambda i: (0, 0)),
            _resident((None, 1, M_HD), lambda i: (layer, 0, 0)),
            _resident((None, 1, M_HD), lambda i: (layer, 0, 0)),
        ],
        out_specs=pl.BlockSpec((ROW_TILE, B_MAIN), lambda i: (i, 0)),
        compiler_params=_params("parallel"),
        name="inproj_swa",
    )(x, pos, mix_g, w_main, gqk, rope_tab, gq, gk)


A_STATE_W = A_DV + LANES


def _mlstm_kernel(q_ref, k_ref, v_ref, og_ref, g_ref, hg_ref, y_ref, ct_sc, m_sc):
    chunk = MLSTM_CHUNK
    scale = A_DQK ** -0.5

    @pl.when(pl.program_id(1) == 0)
    def _():
        ct_sc[...] = jnp.zeros_like(ct_sc)
        m_sc[...] = jnp.zeros_like(m_sc)

    rec = g_ref[...]
    a_t = rec.T
    b = pltpu.roll(rec, LANES - G_LANE_B, axis=1)
    run_max = pltpu.roll(rec, LANES - G_LANE_CM, axis=1)
    m_row = m_sc[0:1, :]
    big_m = jnp.maximum(run_max, m_row)
    inter = scale * jnp.exp(m_row - big_m)
    floor = jnp.exp(-(b + big_m))
    m_last = big_m[chunk - 1:chunk, :]
    decay = jnp.exp(m_row - m_last)
    m_sc[0:1, :] = b[chunk - 1:chunk, :] + m_last
    k_t = k_ref[...].astype(F32).T
    causal = _causal_mask(chunk)
    ones_col = (lax.broadcasted_iota(jnp.int32, (chunk, LANES), 1) == 0).astype(BF16)

    heads = range(A_HEADS)
    qk_cols = [slice(h * A_DQK, (h + 1) * A_DQK) for h in heads]
    v_cols = [slice(h * A_DV, (h + 1) * A_DV) for h in heads]
    for h in heads:
        a_row = a_t[h:h + 1, :]
        decay_mat = jnp.exp(jnp.where(causal, (a_row + math.log(scale)) - big_m[:, h:h + 1], -jnp.inf))
        qh = q_ref[:, qk_cols[h]]
        p = (_dot_nt(qh, k_ref[:, qk_cols[h]]) * decay_mat).astype(BF16)
        q_inter = (qh.astype(F32) * inter[:, h:h + 1]).astype(BF16)
        v_ext = jnp.concatenate([v_ref[:, v_cols[h]], ones_col], axis=1)
        state = ct_sc[h]
        tot = _dot(jnp.concatenate([p, q_inter], axis=1),
                   jnp.concatenate([v_ext, state.astype(BF16)], axis=0))
        num = tot[:, :A_DV]
        den = tot[:, A_DV:A_DV + 1]
        inv = 1.0 / jnp.maximum(jnp.abs(den), floor[:, h:h + 1])
        ms = jnp.mean(num * num, axis=1, keepdims=True)
        fac = inv * lax.rsqrt(inv * inv * ms + EPS)
        y_ref[:, v_cols[h]] = (num * fac * hg_ref[:, v_cols[h]] * og_ref[:, v_cols[h]].astype(F32)).astype(BF16)
        w_row = jnp.exp(a_row - m_last[:, h:h + 1])
        kw_t = (k_t[qk_cols[h], :] * w_row).astype(BF16)
        ct_sc[h] = decay[:, h:h + 1] * state + _dot(kw_t, v_ext)


def _mlstm(main, gates, h_gain, j, bsz, seq):
    nc = seq // MLSTM_CHUNK
    return pl.pallas_call(
        _mlstm_kernel,
        out_shape=jax.ShapeDtypeStruct((bsz * seq, A_V), BF16),
        grid=(bsz, nc),
        in_specs=[
            pl.BlockSpec((MLSTM_CHUNK, A_QK), lambda b, c: (b * nc + c, A_COL_Q // A_QK)),
            pl.BlockSpec((MLSTM_CHUNK, A_QK), lambda b, c: (b * nc + c, A_COL_K // A_QK)),
            pl.BlockSpec((MLSTM_CHUNK, A_V), lambda b, c: (b * nc + c, A_COL_V // A_V)),
            pl.BlockSpec((MLSTM_CHUNK, A_V), lambda b, c: (b * nc + c, A_COL_O // A_V)),
            pl.BlockSpec((MLSTM_CHUNK, LANES), lambda b, c: (b * nc + c, 0)),
            _resident((None, 1, A_V), lambda b, c: (j, 0, 0)),
        ],
        out_specs=pl.BlockSpec((MLSTM_CHUNK, A_V), lambda b, c: (b * nc + c, 0)),
        scratch_shapes=[pltpu.VMEM((A_HEADS, A_DQK, A_STATE_W), F32),
                        pltpu.VMEM((SUBLANES, LANES), F32)],
        compiler_params=_params("parallel", "arbitrary"),
        name="mlstm",
    )(main, main, main, main, gates, h_gain)


SWA_SUB = B_HD
SWA_SPAN = B_WINDOW + SWA_SUB


def _swa_kernel(sink_ref, q_ref, kvp_ref, kvc_ref, y_ref):
    sub_q, span_k = SWA_SUB, SWA_SPAN
    pairs = B_GROUP // 2
    kv = jnp.concatenate([kvp_ref[...], kvc_ref[...]], axis=0).astype(F32)
    keys = kv[:, :B_KV]
    vals = kv[:, B_KV:]
    lane_kv = lax.broadcasted_iota(jnp.int32, keys.shape, 1)
    lane_q = lax.broadcasted_iota(jnp.int32, (sub_q, LANES), 1)
    kk = lax.broadcasted_iota(jnp.int32, (span_k, LANES), 0)
    qq = lax.broadcasted_iota(jnp.int32, (span_k, LANES), 1) % sub_q
    dist = B_WINDOW + qq - kk
    in_window = (dist >= 0) & (dist < B_WINDOW)
    first_step = pl.program_id(1) == 0
    odd_lane = lax.broadcasted_iota(jnp.int32, (1, LANES), 1) >= sub_q

    def both_halves(x, grp):
        own = jnp.where((lane_kv // B_HD) == grp, x, 0.0)
        return own + pltpu.roll(own, B_HD, axis=1)

    kg = [both_halves(keys, grp).astype(BF16) for grp in range(B_KV_HEADS)]
    vg_t = [both_halves(vals, grp).T.astype(BF16) for grp in range(B_KV_HEADS)]
    sink_rows = [jnp.where(odd_lane, sink_ref[2 * p + 1], sink_ref[2 * p]) * LOG2E for p in range(B_HEADS // 2)]

    def scores_of(grp, t):
        slabs = []
        for pp in range(pairs):
            c0 = (grp * pairs + pp) * LANES
            slab = q_ref[t * sub_q:(t + 1) * sub_q, c0:c0 + LANES]
            zero = jnp.zeros_like(slab)
            slabs.append(jnp.where(lane_q < B_HD, slab, zero))
            slabs.append(jnp.where(lane_q >= B_HD, slab, zero))
        return _dot_nt(kg[grp][t * sub_q:t * sub_q + span_k], jnp.concatenate(slabs, axis=0))

    units = [(grp, t) for grp in range(B_KV_HEADS) for t in range(SWA_STEP // sub_q)]
    ahead = scores_of(*units[0])
    for u, (grp, t) in enumerate(units):
        scores_t = ahead
        if u + 1 < len(units):
            ahead = scores_of(*units[u + 1])
        if True:
            rows = slice(t * sub_q, (t + 1) * sub_q)
            span = slice(t * sub_q, t * sub_q + span_k)
            valid = in_window
            if t * sub_q < B_WINDOW:
                valid = valid & ((kk >= B_WINDOW - t * sub_q) | jnp.logical_not(first_step))
            probs = []
            inv = []
            for pp in range(pairs):
                sink = sink_rows[grp * pairs + pp]
                s = jnp.where(valid, scores_t[:, pp * LANES:(pp + 1) * LANES], -jnp.inf)
                m = jnp.maximum(jnp.max(s, axis=0, keepdims=True), sink)
                p = jnp.exp2(s - m)
                inv.append(1.0 / (jnp.sum(p, axis=0, keepdims=True) + jnp.exp2(sink - m)))
                probs.append(p.astype(BF16))
            out_t = _dot(vg_t[grp][:, span], jnp.concatenate(probs, axis=1))
            for pp in range(pairs):
                o = (out_t[:, pp * LANES:(pp + 1) * LANES] * inv[pp]).T
                c0 = (grp * pairs + pp) * LANES
                y_ref[rows, c0:c0 + LANES] = jnp.where(lane_q < B_HD, o[:sub_q], o[sub_q:]).astype(BF16)


def _swa(main, sinks, bsz, seq):
    ns = seq // SWA_STEP
    per_step = SWA_STEP // B_WINDOW
    kv_w = 2 * B_KV
    kv_col = B_COL_KV // kv_w
    return pl.pallas_call(
        _swa_kernel,
        out_shape=jax.ShapeDtypeStruct((bsz * seq, B_Q), BF16),
        grid=(bsz, ns),
        in_specs=[
            pl.BlockSpec(memory_space=pltpu.SMEM),
            pl.BlockSpec((SWA_STEP, B_Q), lambda b, n: (b * ns + n, 0)),
            pl.BlockSpec((B_WINDOW, kv_w),
                         lambda b, n: ((b * ns + n) * per_step - jnp.minimum(n, 1), kv_col)),
            pl.BlockSpec((SWA_STEP, kv_w), lambda b, n: (b * ns + n, kv_col)),
        ],
        out_specs=pl.BlockSpec((SWA_STEP, B_Q), lambda b, n: (b * ns + n, 0)),
        compiler_params=_params("parallel", "arbitrary"),
        name="swa",
    )(sinks, main, main, main)


def _outproj_kernel(x_ref, yt_ref, xq_ref, mk_ref, mvt_ref, w_ref, o_ref):
    y_dim = yt_ref.shape[1]
    head_cols = [slice(h * M_HD, (h + 1) * M_HD) for h in range(M_HEADS)]
    scores = [_dot_nt(mk_ref[:, sl], xq_ref[:, sl]) for sl in head_cols]
    acc = x_ref[...] + _dot(yt_ref[...], w_ref[:y_dim, :])
    heads = []
    for s, sl in zip(scores, head_cols):
        p = jnp.exp(s - jnp.max(s, axis=0, keepdims=True))
        inv = 1.0 / jnp.sum(p, axis=0, keepdims=True)
        heads.append(((_dot(mvt_ref[sl, :], p.astype(BF16)) * inv).T).astype(BF16))
    o_ref[...] = acc + _dot(jnp.concatenate(heads, axis=1), w_ref[y_dim:, :])


def _outproj(x, y_tok, main, xq_col, mem_k, mem_v, w_out, j, bsz, seq):
    nt = seq // ROW_TILE
    y_dim = y_tok.shape[1]
    return pl.pallas_call(
        _outproj_kernel,
        out_shape=jax.ShapeDtypeStruct((bsz * seq, D_MODEL), F32),
        grid=(bsz, nt),
        in_specs=[
            pl.BlockSpec((ROW_TILE, D_MODEL), lambda b, t: (b * nt + t, 0)),
            pl.BlockSpec((ROW_TILE, y_dim), lambda b, t: (b * nt + t, 0)),
            pl.BlockSpec((ROW_TILE, M_Q), lambda b, t: (b * nt + t, xq_col // M_Q)),
            pl.BlockSpec((None, N_MEM, M_Q), lambda b, t: (b, 0, 0)),
            pl.BlockSpec((None, M_Q, N_MEM), lambda b, t: (b, 0, 0)),
            _resident((None, y_dim + M_Q, D_MODEL), lambda b, t: (j, 0, 0)),
        ],
        out_specs=pl.BlockSpec((ROW_TILE, D_MODEL), lambda b, t: (b * nt + t, 0)),
        compiler_params=_params("parallel", "parallel"),
        name="outproj",
    )(x, y_tok, main, mem_k, mem_v, w_out)


def _rope_row_tables():
    inv_freq = ROPE_THETA ** (-np.arange(0, ROT_DIM, 2, dtype=np.float32) / ROT_DIM)
    tab = np.zeros((ROT_DIM, 2 * LANES), np.float32)
    tab[:, :LANES] = np.concatenate([inv_freq, inv_freq])[:, None]
    tab[:, LANES:] = np.where(np.arange(ROT_DIM) < ROT_HALF, -1.0, 1.0)[:, None]
    return jnp.asarray(tab)


def _rope_gain_rows(gain):
    d = np.arange(B_HD)
    partner = np.where(d < ROT_HALF, d + ROT_HALF, np.where(d < ROT_DIM, d - ROT_HALF, d))
    reps = LANES // B_HD
    return jnp.stack([jnp.tile(gain, (1, reps)), jnp.tile(gain[:, partner], (1, reps))], axis=1)


def kernel(x, mem, positions, mem_norm_g, mem_w_kv, ffn1_norm_g, ffn1_w_in, ffn1_w_out,
           mix_norm_g, ffn2_norm_g, ffn2_w_in, ffn2_w_out, xa_q_norm_g, xa_k_norm_g,
           a_w_in, a_gate_b, a_h_norm_g, a_w_out,
           b_w_in, b_q_norm_g, b_k_norm_g, b_sinks, b_w_out):
    bsz, seq, _ = x.shape
    n = bsz * seq
    assert seq % ROW_TILE == 0 and seq % SWA_STEP == 0 and ROW_TILE % MLSTM_CHUNK == 0

    bf = lambda w: w.astype(BF16)
    row3 = lambda g: g.reshape(g.shape[0], 1, g.shape[1])
    ffn1_in, ffn1_out, ffn2_in, ffn2_out = bf(ffn1_w_in), bf(ffn1_w_out), bf(ffn2_w_in), bf(ffn2_w_out)
    a_main = bf(jnp.concatenate([a_w_in[..., :A_COL_XQ], a_w_in[..., A_TOK:]], axis=-1))
    a_gw = bf(jnp.pad(a_w_in[..., A_COL_XQ:A_TOK], ((0, 0), (0, 0), (0, LANES - A_GATES))))
    a_gb = row3(jnp.pad(a_gate_b, ((0, 0), (0, LANES - A_GATES))))
    b_main = bf(jnp.concatenate([b_w_in[..., :B_Q], b_w_in[..., B_TOK:], b_w_in[..., B_Q:B_TOK]], axis=-1))
    b_gqk = jnp.concatenate([_rope_gain_rows(b_q_norm_g), _rope_gain_rows(b_k_norm_g)], axis=1)
    a_out, b_out = bf(a_w_out), bf(b_w_out)
    rope_tab = _rope_row_tables()
    pos = positions.reshape(n // ROW_TILE, 1, ROW_TILE)
    g_ffn1, g_mix, g_ffn2 = row3(ffn1_norm_g), row3(mix_norm_g), row3(ffn2_norm_g)
    g_xq, g_xk, g_h = row3(xa_q_norm_g), row3(xa_k_norm_g), row3(a_h_norm_g)

    mem_k, mem_v = _memkv(mem, mem_norm_g.reshape(1, D_MODEL), bf(mem_w_kv))
    xf = x.reshape(n, D_MODEL)
    for i in range(DEPTH):
        j = i // 2
        xf = _ffn(xf, g_ffn1, ffn1_in, ffn1_out, i)
        if i % 2 == 0:
            main, gates = _inproj_a(xf, g_mix, a_main, a_gw, a_gb, g_xq, g_xk, i, j)
            y_tok = _mlstm(main, gates, g_h, j, bsz, seq)
            xf = _outproj(xf, y_tok, main, A_COL_XQ, mem_k, mem_v, a_out, j, bsz, seq)
        else:
            main = _inproj_b(xf, pos, g_mix, b_main, b_gqk, rope_tab, g_xq, g_xk, i, j)
            y_tok = _swa(main, b_sinks[j], bsz, seq)
            xf = _outproj(xf, y_tok, main, B_COL_XQ, mem_k, mem_v, b_out, j, bsz, seq)
        xf = _ffn(xf, g_ffn2, ffn2_in, ffn2_out, i)
    return xf.reshape(bsz, seq, D_MODEL)
```

```python
import functools
import math

import numpy as np
import jax
import jax.numpy as jnp
from jax import lax
from jax.experimental import pallas as pl
from jax.experimental.pallas import tpu as pltpu

F32 = jnp.float32
BF16 = jnp.bfloat16

D_MODEL = 1024
DEPTH = 4
EPS = 1e-6
D_FF = 2816

A_HEADS = 4
A_DQK = 128
A_DV = 256
A_QK = A_HEADS * A_DQK
A_V = A_HEADS * A_DV
A_GATES = 2 * A_HEADS
A_TOK = 2 * A_QK + 2 * A_V + A_GATES

B_HEADS = 16
B_KV_HEADS = 2
B_GROUP = B_HEADS // B_KV_HEADS
B_HD = 64
B_WINDOW = 128
B_Q = B_HEADS * B_HD
B_KV = B_KV_HEADS * B_HD
B_TOK = B_Q + 2 * B_KV
ROT_DIM = 16
ROT_HALF = ROT_DIM // 2
ROPE_THETA = 500000.0

N_MEM = 256
M_HEADS = 4
M_HD = 128
M_Q = M_HEADS * M_HD

LOG2E = math.log2(math.e)

LANES = 128
SUBLANES = 8
MXU_DIM = 256
VMEM_LIMIT = 56 * 1024 * 1024

ROW_TILE = 512
FF_TILE = MXU_DIM
FFN_ROWS = 1024
FFN_PART = 512
MLSTM_CHUNK = 256
SWA_STEP = 4 * B_WINDOW

A_COL_Q, A_COL_K, A_COL_V, A_COL_O = 0, A_QK, 2 * A_QK, 2 * A_QK + A_V
A_COL_XQ = 2 * A_QK + 2 * A_V
A_MAIN = A_COL_XQ + M_Q
G_LANE_A, G_LANE_B, G_LANE_CM = 0, A_HEADS, 2 * A_HEADS
B_COL_XQ = B_Q
B_COL_KV = B_Q + M_Q
B_MAIN = B_COL_KV + 2 * B_KV


def _params(*semantics):
    return pltpu.CompilerParams(dimension_semantics=semantics, vmem_limit_bytes=VMEM_LIMIT)


def _resident(block_shape, index_map):
    return pl.BlockSpec(block_shape, index_map, pipeline_mode=pl.Buffered(1))


def _rms(x, gain):
    ms = jnp.mean(x * x, axis=-1, keepdims=True)
    return x * lax.rsqrt(ms + EPS) * gain


def _dot(a, b):
    return jnp.dot(a, b, preferred_element_type=F32)


def _dot_nt(a, b):
    return lax.dot_general(a, b, (((1,), (1,)), ((), ())), preferred_element_type=F32)


def _split2(x):
    hi = x.astype(BF16)
    return hi, (x - hi.astype(F32)).astype(BF16)


def _ffn_kernel(x_ref, g_ref, win_ref, wout_ref, o_ref):
    parts = [slice(r * FFN_PART, (r + 1) * FFN_PART) for r in range(FFN_ROWS // FFN_PART)]
    x = [x_ref[rows, :] for rows in parts]
    xn = [_rms(xr, g_ref[...]).astype(BF16) for xr in x]
    acc = [jnp.zeros_like(xr) for xr in x]
    for j in range(D_FF // FF_TILE):
        lo = j * FF_TILE
        for r in range(len(parts)):
            gate = _dot(xn[r], win_ref[:, lo:lo + FF_TILE])
            up = _dot(xn[r], win_ref[:, D_FF + lo:D_FF + lo + FF_TILE])
            h = (gate * jax.nn.sigmoid(gate) * up).astype(BF16)
            acc[r] = acc[r] + _dot(h, wout_ref[lo:lo + FF_TILE, :])
    for r, rows in enumerate(parts):
        o_ref[rows, :] = x[r] + 0.5 * acc[r]


def _ffn(x, gains, w_in, w_out, layer):
    n = x.shape[0]
    return pl.pallas_call(
        _ffn_kernel,
        out_shape=jax.ShapeDtypeStruct((n, D_MODEL), F32),
        grid=(n // FFN_ROWS,),
        in_specs=[
            pl.BlockSpec((FFN_ROWS, D_MODEL), lambda i: (i, 0)),
            _resident((None, 1, D_MODEL), lambda i: (layer, 0, 0)),
            _resident((None, D_MODEL, 2 * D_FF), lambda i: (layer, 0, 0)),
            _resident((None, D_FF, D_MODEL), lambda i: (layer, 0, 0)),
        ],
        out_specs=pl.BlockSpec((FFN_ROWS, D_MODEL), lambda i: (i, 0)),
        compiler_params=_params("parallel"),
        name="ffn",
    )(x, gains, w_in, w_out)


def _memkv_kernel(mem_ref, g_ref, w_ref, k_ref, v_ref):
    mn = _rms(mem_ref[...], g_ref[...]).astype(BF16)
    kv = _dot(mn, w_ref[...])
    for h in range(M_HEADS):
        kh = kv[:, h * M_HD:(h + 1) * M_HD]
        ms = jnp.mean(kh * kh, axis=-1, keepdims=True)
        k_ref[h * M_HD:(h + 1) * M_HD, :] = (kh * lax.rsqrt(ms + EPS)).T.astype(BF16)
    v_ref[...] = kv[:, M_Q:].astype(BF16)


def _memkv(mem, gain, w_kv):
    bsz = mem.shape[0]
    return pl.pallas_call(
        _memkv_kernel,
        out_shape=(jax.ShapeDtypeStruct((bsz, M_Q, N_MEM), BF16),
                   jax.ShapeDtypeStruct((bsz, N_MEM, M_Q), BF16)),
        grid=(bsz,),
        in_specs=[
            pl.BlockSpec((None, N_MEM, D_MODEL), lambda b: (b, 0, 0)),
            _resident((1, D_MODEL), lambda b: (0, 0)),
            _resident((D_MODEL, 2 * M_Q), lambda b: (0, 0)),
        ],
        out_specs=(pl.BlockSpec((None, M_Q, N_MEM), lambda b: (b, 0, 0)),
                   pl.BlockSpec((None, N_MEM, M_Q), lambda b: (b, 0, 0))),
        compiler_params=_params("parallel"),
        name="memkv",
    )(mem, gain, w_kv)


def _store_mem_queries(xq, gq_ref, gk_ref, out_ref, col0):
    geff = gq_ref[...] * gk_ref[...] * (M_HD ** -0.5)
    for h in range(xq.shape[1] // M_HD):
        s = xq[:, h * M_HD:(h + 1) * M_HD]
        ms = jnp.mean(s * s, axis=-1, keepdims=True)
        out_ref[:, col0 + h * M_HD:col0 + (h + 1) * M_HD] = (s * lax.rsqrt(ms + EPS) * geff).astype(BF16)


def _log_sigmoid(x):
    return jnp.minimum(x, 0.0) - jnp.log(1.0 + jnp.exp(-jnp.abs(x)))


def _causal_mask(n):
    row = lax.broadcasted_iota(jnp.int32, (n, n), 0)
    col = lax.broadcasted_iota(jnp.int32, (n, n), 1)
    return col <= row


def _mlstm_gate_record(g):
    chunk = g.shape[0]
    causal = _causal_mask(chunk)
    hi, lo = _split2(_log_sigmoid(g))
    both = _dot(causal.astype(BF16), jnp.concatenate([hi, lo], axis=1))
    b = both[:, :LANES] + both[:, LANES:]
    a = g - pltpu.roll(b, LANES - A_HEADS, axis=1)
    a_t = a.T
    lane = lax.broadcasted_iota(jnp.int32, (chunk, LANES), 1)
    rec = jnp.where(lane < G_LANE_B, a, jnp.where(lane < G_LANE_CM, b, 0.0))
    for h in range(A_HEADS):
        run_max = jnp.max(jnp.where(causal, a_t[h:h + 1, :], -jnp.inf), axis=1, keepdims=True)
        rec = jnp.where(lane == G_LANE_CM + h, run_max, rec)
    return rec


ROPE_PAD = B_HD - ROT_DIM


def _inproj_b_kernel(x_ref, pos_ref, g_ref, w_ref, gqk_ref, tab_ref, gq_ref, gk_ref, main_ref):
    xn = _rms(x_ref[...], g_ref[...]).astype(BF16)
    ang = tab_ref[:, 0:1] * pos_ref[...].astype(F32)
    cos_c = jnp.cos(ang)
    sin_c = jnp.sin(ang) * tab_ref[:, LANES:LANES + 1]
    pad = jnp.zeros((ROPE_PAD, ang.shape[1]), F32)
    cos_t = jnp.concatenate([cos_c, pad, cos_c, pad], axis=0).T
    sin_t = jnp.concatenate([sin_c, pad, sin_c, pad], axis=0).T
    lane = lax.broadcasted_iota(jnp.int32, (1, LANES), 1) % B_HD
    cos_t = cos_t + (lane >= ROT_DIM).astype(F32)
    first_half = lane < ROT_HALF
    r = lax.broadcasted_iota(jnp.int32, (MXU_DIM, MXU_DIM), 0) // B_HD
    c = lax.broadcasted_iota(jnp.int32, (MXU_DIM, MXU_DIM), 1) // B_HD
    seg = jnp.where(r == c, 1.0 / B_HD, 0.0).astype(BF16)

    def norm_rope(slab, cos_g, sin_g, col0):
        hi, lo = _split2(slab * slab)
        inv = lax.rsqrt(_dot(hi, seg) + _dot(lo, seg) + EPS)
        for half in range(MXU_DIM // LANES):
            sl = slice(half * LANES, (half + 1) * LANES)
            xh = slab[:, sl]
            partner = jnp.where(first_half, pltpu.roll(xh, LANES - ROT_HALF, axis=1),
                                pltpu.roll(xh, ROT_HALF, axis=1))
            out = (xh * cos_g + partner * sin_g) * inv[:, sl]
            main_ref[:, col0 + half * LANES:col0 + (half + 1) * LANES] = out.astype(BF16)

    q_scale = B_HD ** -0.5 * LOG2E
    cos_q, sin_q = cos_t * (gqk_ref[0:1, :] * q_scale), sin_t * (gqk_ref[1:2, :] * q_scale)
    cos_k, sin_k = cos_t * gqk_ref[2:3, :], sin_t * gqk_ref[3:4, :]
    n_q = B_Q // MXU_DIM
    slab = _dot(xn, w_ref[:, 0:MXU_DIM])
    for s in range(n_q):
        nxt = (s + 1) * MXU_DIM if s + 1 < n_q else B_COL_KV
        ahead = _dot(xn, w_ref[:, nxt:nxt + MXU_DIM])
        norm_rope(slab, cos_q, sin_q, s * MXU_DIM)
        slab = ahead
    xq = _dot(xn, w_ref[:, B_COL_XQ:B_COL_KV])
    v = slab[:, B_KV:]
    norm_rope(slab, cos_k, sin_k, B_COL_KV)
    main_ref[:, B_COL_KV + B_KV:B_MAIN] = v.astype(BF16)
    _store_mem_queries(xq, gq_ref, gk_ref, main_ref, B_COL_XQ)


def _inproj_b(x, pos, mix_g, w_main, gqk, rope_tab, gq, gk, layer, j):
    n = x.shape[0]
    return pl.pallas_call(
        _inproj_b_kernel,
        out_shape=jax.ShapeDtypeStruct((n, B_MAIN), BF16),
        grid=(n // ROW_TILE,),
        in_specs=[
            pl.BlockSpec((ROW_TILE, D_MODEL), lambda i: (i, 0)),
            pl.BlockSpec((None, 1, ROW_TILE), lambda i: (i, 0, 0)),
            _resident((None, 1, D_MODEL), lambda i: (layer, 0, 0)),
            _resident((None, D_MODEL, B_MAIN), lambda i: (j, 0, 0)),
            _resident((None, 4, LANES), lambda i: (j, 0, 0)),
            _resident((ROT_DIM, 2 * LANES), lambda i: (0, 0)),
            _resident((None, 1, M_HD), lambda i: (layer, 0, 0)),
            _resident((None, 1, M_HD), lambda i: (layer, 0, 0)),
        ],
        out_specs=pl.BlockSpec((ROW_TILE, B_MAIN), lambda i: (i, 0)),
        compiler_params=_params("parallel"),
        name="inproj_swa",
    )(x, pos, mix_g, w_main, gqk, rope_tab, gq, gk)


A_STATE_W = A_DV + LANES


def _mlstm_chunk_setup(rec, k, m_sc):
    chunk = rec.shape[0]
    scale = A_DQK ** -0.5
    b = pltpu.roll(rec, LANES - G_LANE_B, axis=1)
    run_max = pltpu.roll(rec, LANES - G_LANE_CM, axis=1)
    m_row = m_sc[0:1, :]
    big_m = jnp.maximum(run_max, m_row)
    m_last = big_m[chunk - 1:chunk, :]
    m_sc[0:1, :] = b[chunk - 1:chunk, :] + m_last
    return dict(
        a_t=rec.T,
        big_m=big_m,
        inter=scale * jnp.exp(m_row - big_m),
        floor=jnp.exp(-(b + big_m)),
        m_last=m_last,
        decay=jnp.exp(m_row - m_last),
        k_t=k.astype(F32).T,
        causal=_causal_mask(chunk),
        ones_col=(lax.broadcasted_iota(jnp.int32, (chunk, LANES), 1) == 0).astype(BF16),
    )


def _mlstm_head(h, c, qh, vh, og, gain, ct_sc, between):
    scale = A_DQK ** -0.5
    a_row = c["a_t"][h:h + 1, :]
    k_t = c["k_t"][h * A_DQK:(h + 1) * A_DQK, :]
    qk = _dot(qh, k_t.astype(BF16))
    v_ext = jnp.concatenate([vh, c["ones_col"]], axis=1)
    state = ct_sc[h]
    w_row = jnp.exp(a_row - c["m_last"][:, h:h + 1])
    kw_t = (k_t * w_row).astype(BF16)
    ct_sc[h] = c["decay"][:, h:h + 1] * state + _dot(kw_t, v_ext)
    between()
    big_m = c["big_m"][:, h:h + 1]
    decay_mat = jnp.exp(jnp.where(c["causal"], (a_row + math.log(scale)) - big_m, -jnp.inf))
    p = (qk * decay_mat).astype(BF16)
    q_inter = (qh.astype(F32) * c["inter"][:, h:h + 1]).astype(BF16)
    tot = _dot(jnp.concatenate([p, q_inter], axis=1),
               jnp.concatenate([v_ext, state.astype(BF16)], axis=0))
    num = tot[:, :A_DV]
    den = tot[:, A_DV:A_DV + 1]
    inv = 1.0 / jnp.maximum(jnp.abs(den), c["floor"][:, h:h + 1])
    ms = jnp.mean(num * num, axis=1, keepdims=True)
    fac = inv * lax.rsqrt(inv * inv * ms + EPS)
    return (num * fac * gain * og.astype(F32)).astype(BF16)


A_PROJ_W = A_COL_XQ
A_UNIT_W = MXU_DIM


def _mlstm_layer_kernel(blocks_per_row, x_ref, g_ref, w_ref, wg_ref, gb_ref, gq_ref, gk_ref, hg_ref,
                        xq_ref, y_ref, proj_sc, rec_sc, ct_sc, m_sc):
    s = pl.program_id(0)

    @pl.when(s == 0)
    def _():
        proj_sc[...] = jnp.zeros_like(proj_sc)
        rec_sc[...] = jnp.zeros_like(rec_sc)

    @pl.when(lax.rem(jnp.maximum(s - 1, 0), blocks_per_row) == 0)
    def _():
        ct_sc[...] = jnp.zeros_like(ct_sc)
        m_sc[...] = jnp.zeros_like(m_sc)

    put = proj_sc.at[lax.rem(s, 2)]
    put_rec = rec_sc.at[lax.rem(s, 2)]
    get = proj_sc.at[lax.rem(s + 1, 2)]
    get_rec = rec_sc.at[lax.rem(s + 1, 2)]

    xn = _rms(x_ref[...], g_ref[...]).astype(BF16)

    def project(c):
        cols = slice(c * A_UNIT_W, (c + 1) * A_UNIT_W)
        y = _dot(xn, w_ref[:, cols])
        if cols.start >= A_COL_XQ:
            _store_mem_queries(y, gq_ref, gk_ref, xq_ref, cols.start - A_COL_XQ)
            return
        if cols.start >= A_COL_O:
            y = jax.nn.sigmoid(y)
        put[:, cols] = y.astype(BF16)

    chunks = [slice(c * MLSTM_CHUNK, (c + 1) * MLSTM_CHUNK) for c in range(ROW_TILE // MLSTM_CHUNK)]
    setups = {}

    def recur(c, h, between):
        rows = chunks[c]
        if h == 0:
            setups[c] = _mlstm_chunk_setup(get_rec[rows, :], get[rows, A_COL_K:A_COL_K + A_QK], m_sc)
        qk = slice(h * A_DQK, (h + 1) * A_DQK)
        v = slice(h * A_DV, (h + 1) * A_DV)
        y_ref[rows, v] = _mlstm_head(
            h, setups[c], get[rows, A_COL_Q + qk.start:A_COL_Q + qk.stop],
            get[rows, A_COL_V + v.start:A_COL_V + v.stop],
            get[rows, A_COL_O + v.start:A_COL_O + v.stop], hg_ref[:, v], ct_sc, between)

    gates = _dot(xn, wg_ref[...]) + gb_ref[...]
    project(0)
    project(1)
    for rows in chunks:
        put_rec[rows, :] = _mlstm_gate_record(gates[rows])
    todo = list(range(2, A_MAIN // A_UNIT_W))
    heads = [(c, h) for c in range(len(chunks)) for h in range(A_HEADS)]
    base, extra = divmod(len(todo), len(heads))
    for i, (c, h) in enumerate(heads):
        share = [todo.pop(0) for _ in range(base + (i < extra))]
        recur(c, h, lambda share=share: [project(u) for u in share])


def _mlstm_layer(x, mix_g, w_main, w_gates, gate_b, gq, gk, h_gain, layer, j, bsz, seq):
    n = bsz * seq
    nb = n // ROW_TILE
    last = nb - 1
    return pl.pallas_call(
        functools.partial(_mlstm_layer_kernel, seq // ROW_TILE),
        out_shape=(jax.ShapeDtypeStruct((n, M_Q), BF16),
                   jax.ShapeDtypeStruct((n, A_V), BF16)),
        grid=(nb + 1,),
        in_specs=[
            pl.BlockSpec((ROW_TILE, D_MODEL), lambda s: (jnp.minimum(s, last), 0)),
            _resident((None, 1, D_MODEL), lambda s: (layer, 0, 0)),
            _resident((None, D_MODEL, A_MAIN), lambda s: (j, 0, 0)),
            _resident((None, D_MODEL, LANES), lambda s: (j, 0, 0)),
            _resident((None, 1, LANES), lambda s: (j, 0, 0)),
            _resident((None, 1, M_HD), lambda s: (layer, 0, 0)),
            _resident((None, 1, M_HD), lambda s: (layer, 0, 0)),
            _resident((None, 1, A_V), lambda s: (j, 0, 0)),
        ],
        out_specs=(pl.BlockSpec((ROW_TILE, M_Q), lambda s: (jnp.minimum(s, last), 0)),
                   pl.BlockSpec((ROW_TILE, A_V), lambda s: (jnp.maximum(s - 1, 0), 0))),
        scratch_shapes=[pltpu.VMEM((2, ROW_TILE, A_PROJ_W), BF16),
                        pltpu.VMEM((2, ROW_TILE, LANES), F32),
                        pltpu.VMEM((A_HEADS, A_DQK, A_STATE_W), F32),
                        pltpu.VMEM((SUBLANES, LANES), F32)],
        compiler_params=_params("arbitrary"),
        name="mlstm_layer",
    )(x, mix_g, w_main, w_gates, gate_b, gq, gk, h_gain)


SWA_SUB = B_HD
SWA_SPAN = B_WINDOW + SWA_SUB


def _swa_kernel(sink_ref, q_ref, kvp_ref, kvc_ref, y_ref):
    sub_q, span_k = SWA_SUB, SWA_SPAN
    pairs = B_GROUP // 2
    kv = jnp.concatenate([kvp_ref[...], kvc_ref[...]], axis=0).astype(F32)
    keys = kv[:, :B_KV]
    vals = kv[:, B_KV:]
    lane_kv = lax.broadcasted_iota(jnp.int32, keys.shape, 1)
    lane_q = lax.broadcasted_iota(jnp.int32, (sub_q, LANES), 1)
    kk = lax.broadcasted_iota(jnp.int32, (span_k, LANES), 0)
    qq = lax.broadcasted_iota(jnp.int32, (span_k, LANES), 1) % sub_q
    dist = B_WINDOW + qq - kk
    in_window = (dist >= 0) & (dist < B_WINDOW)
    first_step = pl.program_id(1) == 0
    odd_lane = lax.broadcasted_iota(jnp.int32, (1, LANES), 1) >= sub_q

    def both_halves(x, grp):
        own = jnp.where((lane_kv // B_HD) == grp, x, 0.0)
        return own + pltpu.roll(own, B_HD, axis=1)

    kg = [both_halves(keys, grp).astype(BF16) for grp in range(B_KV_HEADS)]
    vg_t = [both_halves(vals, grp).T.astype(BF16) for grp in range(B_KV_HEADS)]
    sink_rows = [jnp.where(odd_lane, sink_ref[2 * p + 1], sink_ref[2 * p]) * LOG2E for p in range(B_HEADS // 2)]

    def scores_of(grp, t):
        slabs = []
        for pp in range(pairs):
            c0 = (grp * pairs + pp) * LANES
            slab = q_ref[t * sub_q:(t + 1) * sub_q, c0:c0 + LANES]
            zero = jnp.zeros_like(slab)
            slabs.append(jnp.where(lane_q < B_HD, slab, zero))
            slabs.append(jnp.where(lane_q >= B_HD, slab, zero))
        return _dot_nt(kg[grp][t * sub_q:t * sub_q + span_k], jnp.concatenate(slabs, axis=0))

    units = [(grp, t) for grp in range(B_KV_HEADS) for t in range(SWA_STEP // sub_q)]
    ahead = scores_of(*units[0])
    for u, (grp, t) in enumerate(units):
        scores_t = ahead
        if u + 1 < len(units):
            ahead = scores_of(*units[u + 1])
        if True:
            rows = slice(t * sub_q, (t + 1) * sub_q)
            span = slice(t * sub_q, t * sub_q + span_k)
            valid = in_window
            if t * sub_q < B_WINDOW:
                valid = valid & ((kk >= B_WINDOW - t * sub_q) | jnp.logical_not(first_step))
            probs = []
            inv = []
            for pp in range(pairs):
                sink = sink_rows[grp * pairs + pp]
                s = jnp.where(valid, scores_t[:, pp * LANES:(pp + 1) * LANES], -jnp.inf)
                m = jnp.maximum(jnp.max(s, axis=0, keepdims=True), sink)
                p = jnp.exp2(s - m)
                inv.append(1.0 / (jnp.sum(p, axis=0, keepdims=True) + jnp.exp2(sink - m)))
                probs.append(p.astype(BF16))
            out_t = _dot(vg_t[grp][:, span], jnp.concatenate(probs, axis=1))
            for pp in range(pairs):
                o = (out_t[:, pp * LANES:(pp + 1) * LANES] * inv[pp]).T
                c0 = (grp * pairs + pp) * LANES
                y_ref[rows, c0:c0 + LANES] = jnp.where(lane_q < B_HD, o[:sub_q], o[sub_q:]).astype(BF16)


def _swa(main, sinks, bsz, seq):
    ns = seq // SWA_STEP
    per_step = SWA_STEP // B_WINDOW
    kv_w = 2 * B_KV
    kv_col = B_COL_KV // kv_w
    return pl.pallas_call(
        _swa_kernel,
        out_shape=jax.ShapeDtypeStruct((bsz * seq, B_Q), BF16),
        grid=(bsz, ns),
        in_specs=[
            pl.BlockSpec(memory_space=pltpu.SMEM),
            pl.BlockSpec((SWA_STEP, B_Q), lambda b, n: (b * ns + n, 0)),
            pl.BlockSpec((B_WINDOW, kv_w),
                         lambda b, n: ((b * ns + n) * per_step - jnp.minimum(n, 1), kv_col)),
            pl.BlockSpec((SWA_STEP, kv_w), lambda b, n: (b * ns + n, kv_col)),
        ],
        out_specs=pl.BlockSpec((SWA_STEP, B_Q), lambda b, n: (b * ns + n, 0)),
        compiler_params=_params("parallel", "arbitrary"),
        name="swa",
    )(sinks, main, main, main)


def _outproj_kernel(x_ref, yt_ref, xq_ref, mkt_ref, mv_ref, w_ref, o_ref):
    y_dim = yt_ref.shape[1]
    head_cols = [slice(h * M_HD, (h + 1) * M_HD) for h in range(M_HEADS)]
    scores = [_dot(xq_ref[:, sl], mkt_ref[sl, :]) for sl in head_cols]
    acc = x_ref[...] + _dot(yt_ref[...], w_ref[:y_dim, :])
    heads = []
    for s, sl in zip(scores, head_cols):
        p = jnp.exp(s - jnp.max(s, axis=1, keepdims=True))
        inv = 1.0 / jnp.sum(p, axis=1, keepdims=True)
        heads.append((_dot(p.astype(BF16), mv_ref[:, sl]) * inv).astype(BF16))
    o_ref[...] = acc + _dot(jnp.concatenate(heads, axis=1), w_ref[y_dim:, :])


def _outproj(x, y_tok, main, xq_col, mem_k, mem_v, w_out, j, bsz, seq):
    nt = seq // ROW_TILE
    y_dim = y_tok.shape[1]
    return pl.pallas_call(
        _outproj_kernel,
        out_shape=jax.ShapeDtypeStruct((bsz * seq, D_MODEL), F32),
        grid=(bsz, nt),
        in_specs=[
            pl.BlockSpec((ROW_TILE, D_MODEL), lambda b, t: (b * nt + t, 0)),
            pl.BlockSpec((ROW_TILE, y_dim), lambda b, t: (b * nt + t, 0)),
            pl.BlockSpec((ROW_TILE, M_Q), lambda b, t: (b * nt + t, xq_col // M_Q)),
            pl.BlockSpec((None, M_Q, N_MEM), lambda b, t: (b, 0, 0)),
            pl.BlockSpec((None, N_MEM, M_Q), lambda b, t: (b, 0, 0)),
            _resident((None, y_dim + M_Q, D_MODEL), lambda b, t: (j, 0, 0)),
        ],
        out_specs=pl.BlockSpec((ROW_TILE, D_MODEL), lambda b, t: (b * nt + t, 0)),
        compiler_params=_params("parallel", "parallel"),
        name="outproj",
    )(x, y_tok, main, mem_k, mem_v, w_out)


def _rope_row_tables():
    inv_freq = ROPE_THETA ** (-np.arange(0, ROT_DIM, 2, dtype=np.float32) / ROT_DIM)
    tab = np.zeros((ROT_DIM, 2 * LANES), np.float32)
    tab[:, :LANES] = np.concatenate([inv_freq, inv_freq])[:, None]
    tab[:, LANES:] = np.where(np.arange(ROT_DIM) < ROT_HALF, -1.0, 1.0)[:, None]
    return jnp.asarray(tab)


def _rope_gain_rows(gain):
    d = np.arange(B_HD)
    partner = np.where(d < ROT_HALF, d + ROT_HALF, np.where(d < ROT_DIM, d - ROT_HALF, d))
    reps = LANES // B_HD
    return jnp.stack([jnp.tile(gain, (1, reps)), jnp.tile(gain[:, partner], (1, reps))], axis=1)


def kernel(x, mem, positions, mem_norm_g, mem_w_kv, ffn1_norm_g, ffn1_w_in, ffn1_w_out,
           mix_norm_g, ffn2_norm_g, ffn2_w_in, ffn2_w_out, xa_q_norm_g, xa_k_norm_g,
           a_w_in, a_gate_b, a_h_norm_g, a_w_out,
           b_w_in, b_q_norm_g, b_k_norm_g, b_sinks, b_w_out):
    bsz, seq, _ = x.shape
    n = bsz * seq
    assert seq % ROW_TILE == 0 and seq % SWA_STEP == 0 and ROW_TILE % MLSTM_CHUNK == 0

    bf = lambda w: w.astype(BF16)
    row3 = lambda g: g.reshape(g.shape[0], 1, g.shape[1])
    ffn1_in, ffn1_out, ffn2_in, ffn2_out = bf(ffn1_w_in), bf(ffn1_w_out), bf(ffn2_w_in), bf(ffn2_w_out)
    a_main = bf(jnp.concatenate([a_w_in[..., :A_COL_XQ], a_w_in[..., A_TOK:]], axis=-1))
    a_gw = bf(jnp.pad(a_w_in[..., A_COL_XQ:A_TOK], ((0, 0), (0, 0), (0, LANES - A_GATES))))
    a_gb = row3(jnp.pad(a_gate_b, ((0, 0), (0, LANES - A_GATES))))
    b_main = bf(jnp.concatenate([b_w_in[..., :B_Q], b_w_in[..., B_TOK:], b_w_in[..., B_Q:B_TOK]], axis=-1))
    b_gqk = jnp.concatenate([_rope_gain_rows(b_q_norm_g), _rope_gain_rows(b_k_norm_g)], axis=1)
    a_out, b_out = bf(a_w_out), bf(b_w_out)
    rope_tab = _rope_row_tables()
    pos = positions.reshape(n // ROW_TILE, 1, ROW_TILE)
    g_ffn1, g_mix, g_ffn2 = row3(ffn1_norm_g), row3(mix_norm_g), row3(ffn2_norm_g)
    g_xq, g_xk, g_h = row3(xa_q_norm_g), row3(xa_k_norm_g), row3(a_h_norm_g)

    mem_k, mem_v = _memkv(mem, mem_norm_g.reshape(1, D_MODEL), bf(mem_w_kv))
    xf = x.reshape(n, D_MODEL)
    for i in range(DEPTH):
        j = i // 2
        xf = _ffn(xf, g_ffn1, ffn1_in, ffn1_out, i)
        if i % 2 == 0:
            xq, y_tok = _mlstm_layer(xf, g_mix, a_main, a_gw, a_gb, g_xq, g_xk, g_h, i, j, bsz, seq)
            xf = _outproj(xf, y_tok, xq, 0, mem_k, mem_v, a_out, j, bsz, seq)
        else:
            main = _inproj_b(xf, pos, g_mix, b_main, b_gqk, rope_tab, g_xq, g_xk, i, j)
            y_tok = _swa(main, b_sinks[j], bsz, seq)
            xf = _outproj(xf, y_tok, main, B_COL_XQ, mem_k, mem_v, b_out, j, bsz, seq)
        xf = _ffn(xf, g_ffn2, ffn2_in, ffn2_out, i)
    return xf.reshape(bsz, seq, D_MODEL)
```

```python
import functools
import math

import numpy as np
import jax
import jax.numpy as jnp
from jax import lax
from jax.experimental import pallas as pl
from jax.experimental.pallas import tpu as pltpu

F32 = jnp.float32
BF16 = jnp.bfloat16

D_MODEL = 1024
DEPTH = 4
EPS = 1e-6
D_FF = 2816

A_HEADS = 4
A_DQK = 128
A_DV = 256
A_QK = A_HEADS * A_DQK
A_V = A_HEADS * A_DV
A_GATES = 2 * A_HEADS
A_TOK = 2 * A_QK + 2 * A_V + A_GATES

B_HEADS = 16
B_KV_HEADS = 2
B_GROUP = B_HEADS // B_KV_HEADS
B_HD = 64
B_WINDOW = 128
B_Q = B_HEADS * B_HD
B_KV = B_KV_HEADS * B_HD
B_TOK = B_Q + 2 * B_KV
ROT_DIM = 16
ROT_HALF = ROT_DIM // 2
ROPE_THETA = 500000.0

N_MEM = 256
M_HEADS = 4
M_HD = 128
M_Q = M_HEADS * M_HD

LOG2E = math.log2(math.e)

LANES = 128
SUBLANES = 8
MXU_DIM = 256
VMEM_LIMIT = 56 * 1024 * 1024

ROW_TILE = 512
FF_TILE = MXU_DIM
FFN_ROWS = 1024
FFN_PART = 512
MLSTM_CHUNK = 256

A_COL_Q, A_COL_K, A_COL_V, A_COL_O = 0, A_QK, 2 * A_QK, 2 * A_QK + A_V
A_COL_XQ = 2 * A_QK + 2 * A_V
A_MAIN = A_COL_XQ + M_Q
G_LANE_A, G_LANE_B, G_LANE_CM = 0, A_HEADS, 2 * A_HEADS
B_COL_XQ = B_Q
B_COL_KV = B_Q + M_Q
B_MAIN = B_COL_KV + 2 * B_KV


def _params(*semantics):
    return pltpu.CompilerParams(dimension_semantics=semantics, vmem_limit_bytes=VMEM_LIMIT)


def _resident(block_shape, index_map):
    return pl.BlockSpec(block_shape, index_map, pipeline_mode=pl.Buffered(1))


def _rms(x, gain):
    ms = jnp.mean(x * x, axis=-1, keepdims=True)
    return x * lax.rsqrt(ms + EPS) * gain


def _dot(a, b):
    return jnp.dot(a, b, preferred_element_type=F32)


def _dot_nt(a, b):
    return lax.dot_general(a, b, (((1,), (1,)), ((), ())), preferred_element_type=F32)


def _split2(x):
    hi = x.astype(BF16)
    return hi, (x - hi.astype(F32)).astype(BF16)


def _ffn_kernel(x_ref, g_ref, win_ref, wout_ref, o_ref):
    parts = [slice(r * FFN_PART, (r + 1) * FFN_PART) for r in range(FFN_ROWS // FFN_PART)]
    x = [x_ref[rows, :] for rows in parts]
    xn = [_rms(xr, g_ref[...]).astype(BF16) for xr in x]
    acc = [jnp.zeros_like(xr) for xr in x]
    for j in range(D_FF // FF_TILE):
        lo = j * FF_TILE
        for r in range(len(parts)):
            gate = _dot(xn[r], win_ref[:, lo:lo + FF_TILE])
            up = _dot(xn[r], win_ref[:, D_FF + lo:D_FF + lo + FF_TILE])
            h = (gate * jax.nn.sigmoid(gate) * up).astype(BF16)
            acc[r] = acc[r] + _dot(h, wout_ref[lo:lo + FF_TILE, :])
    for r, rows in enumerate(parts):
        o_ref[rows, :] = x[r] + 0.5 * acc[r]


def _ffn(x, gains, w_in, w_out, layer):
    n = x.shape[0]
    return pl.pallas_call(
        _ffn_kernel,
        out_shape=jax.ShapeDtypeStruct((n, D_MODEL), F32),
        grid=(n // FFN_ROWS,),
        in_specs=[
            pl.BlockSpec((FFN_ROWS, D_MODEL), lambda i: (i, 0)),
            _resident((None, 1, D_MODEL), lambda i: (layer, 0, 0)),
            _resident((None, D_MODEL, 2 * D_FF), lambda i: (layer, 0, 0)),
            _resident((None, D_FF, D_MODEL), lambda i: (layer, 0, 0)),
        ],
        out_specs=pl.BlockSpec((FFN_ROWS, D_MODEL), lambda i: (i, 0)),
        compiler_params=_params("parallel"),
        name="ffn",
    )(x, gains, w_in, w_out)


def _memkv_kernel(mem_ref, g_ref, w_ref, k_ref, v_ref):
    mn = _rms(mem_ref[...], g_ref[...]).astype(BF16)
    kv = _dot(mn, w_ref[...])
    for h in range(M_HEADS):
        kh = kv[:, h * M_HD:(h + 1) * M_HD]
        ms = jnp.mean(kh * kh, axis=-1, keepdims=True)
        k_ref[h * M_HD:(h + 1) * M_HD, :] = (kh * lax.rsqrt(ms + EPS)).T.astype(BF16)
    v_ref[...] = kv[:, M_Q:].astype(BF16)


def _memkv(mem, gain, w_kv):
    bsz = mem.shape[0]
    return pl.pallas_call(
        _memkv_kernel,
        out_shape=(jax.ShapeDtypeStruct((bsz, M_Q, N_MEM), BF16),
                   jax.ShapeDtypeStruct((bsz, N_MEM, M_Q), BF16)),
        grid=(bsz,),
        in_specs=[
            pl.BlockSpec((None, N_MEM, D_MODEL), lambda b: (b, 0, 0)),
            _resident((1, D_MODEL), lambda b: (0, 0)),
            _resident((D_MODEL, 2 * M_Q), lambda b: (0, 0)),
        ],
        out_specs=(pl.BlockSpec((None, M_Q, N_MEM), lambda b: (b, 0, 0)),
                   pl.BlockSpec((None, N_MEM, M_Q), lambda b: (b, 0, 0))),
        compiler_params=_params("parallel"),
        name="memkv",
    )(mem, gain, w_kv)


def _store_mem_queries(xq, gq_ref, gk_ref, out_ref, col0):
    geff = gq_ref[...] * gk_ref[...] * (M_HD ** -0.5)
    for h in range(xq.shape[1] // M_HD):
        s = xq[:, h * M_HD:(h + 1) * M_HD]
        ms = jnp.mean(s * s, axis=-1, keepdims=True)
        out_ref[:, col0 + h * M_HD:col0 + (h + 1) * M_HD] = (s * lax.rsqrt(ms + EPS) * geff).astype(BF16)


def _log_sigmoid(x):
    return jnp.minimum(x, 0.0) - jnp.log(1.0 + jnp.exp(-jnp.abs(x)))


def _causal_mask(n):
    row = lax.broadcasted_iota(jnp.int32, (n, n), 0)
    col = lax.broadcasted_iota(jnp.int32, (n, n), 1)
    return col <= row


def _mlstm_gate_record(g):
    chunk = g.shape[0]
    causal = _causal_mask(chunk)
    hi, lo = _split2(_log_sigmoid(g))
    both = _dot(causal.astype(BF16), jnp.concatenate([hi, lo], axis=1))
    b = both[:, :LANES] + both[:, LANES:]
    a = g - pltpu.roll(b, LANES - A_HEADS, axis=1)
    a_t = a.T
    lane = lax.broadcasted_iota(jnp.int32, (chunk, LANES), 1)
    rec = jnp.where(lane < G_LANE_B, a, jnp.where(lane < G_LANE_CM, b, 0.0))
    for h in range(A_HEADS):
        run_max = jnp.max(jnp.where(causal, a_t[h:h + 1, :], -jnp.inf), axis=1, keepdims=True)
        rec = jnp.where(lane == G_LANE_CM + h, run_max, rec)
    return rec


ROPE_PAD = B_HD - ROT_DIM


def _rope_projector(pos_ref, gqk_ref, tab_ref, out_ref):
    ang = tab_ref[:, 0:1] * pos_ref[...].astype(F32)
    cos_c = jnp.cos(ang)
    sin_c = jnp.sin(ang) * tab_ref[:, LANES:LANES + 1]
    pad = jnp.zeros((ROPE_PAD, ang.shape[1]), F32)
    cos_t = jnp.concatenate([cos_c, pad, cos_c, pad], axis=0).T
    sin_t = jnp.concatenate([sin_c, pad, sin_c, pad], axis=0).T
    lane = lax.broadcasted_iota(jnp.int32, (1, LANES), 1) % B_HD
    cos_t = cos_t + (lane >= ROT_DIM).astype(F32)
    first_half = lane < ROT_HALF
    r = lax.broadcasted_iota(jnp.int32, (MXU_DIM, MXU_DIM), 0) // B_HD
    c = lax.broadcasted_iota(jnp.int32, (MXU_DIM, MXU_DIM), 1) // B_HD
    seg = jnp.where(r == c, 1.0 / B_HD, 0.0).astype(BF16)

    def norm_rope(slab, cos_g, sin_g, col0):
        hi, lo = _split2(slab * slab)
        inv = lax.rsqrt(_dot(hi, seg) + _dot(lo, seg) + EPS)
        for half in range(MXU_DIM // LANES):
            sl = slice(half * LANES, (half + 1) * LANES)
            xh = slab[:, sl]
            partner = jnp.where(first_half, pltpu.roll(xh, LANES - ROT_HALF, axis=1),
                                pltpu.roll(xh, ROT_HALF, axis=1))
            out = (xh * cos_g + partner * sin_g) * inv[:, sl]
            out_ref[:, col0 + half * LANES:col0 + (half + 1) * LANES] = out.astype(BF16)

    q_scale = B_HD ** -0.5 * LOG2E
    cos_q, sin_q = cos_t * (gqk_ref[0:1, :] * q_scale), sin_t * (gqk_ref[1:2, :] * q_scale)
    cos_k, sin_k = cos_t * gqk_ref[2:3, :], sin_t * gqk_ref[3:4, :]
    return norm_rope, (cos_q, sin_q, cos_k, sin_k)


A_STATE_W = A_DV + LANES


def _mlstm_chunk_setup(rec, k, m_sc):
    chunk = rec.shape[0]
    scale = A_DQK ** -0.5
    b = pltpu.roll(rec, LANES - G_LANE_B, axis=1)
    run_max = pltpu.roll(rec, LANES - G_LANE_CM, axis=1)
    m_row = m_sc[0:1, :]
    big_m = jnp.maximum(run_max, m_row)
    m_last = big_m[chunk - 1:chunk, :]
    m_sc[0:1, :] = b[chunk - 1:chunk, :] + m_last
    return dict(
        a_t=rec.T,
        big_m=big_m,
        inter=scale * jnp.exp(m_row - big_m),
        floor=jnp.exp(-(b + big_m)),
        m_last=m_last,
        decay=jnp.exp(m_row - m_last),
        k_t=k.astype(F32).T,
        causal=_causal_mask(chunk),
        ones_col=(lax.broadcasted_iota(jnp.int32, (chunk, LANES), 1) == 0).astype(BF16),
    )


def _mlstm_head(h, c, qh, vh, og, gain, ct_sc, between):
    scale = A_DQK ** -0.5
    a_row = c["a_t"][h:h + 1, :]
    k_t = c["k_t"][h * A_DQK:(h + 1) * A_DQK, :]
    qk = _dot(qh, k_t.astype(BF16))
    v_ext = jnp.concatenate([vh, c["ones_col"]], axis=1)
    state = ct_sc[h]
    w_row = jnp.exp(a_row - c["m_last"][:, h:h + 1])
    kw_t = (k_t * w_row).astype(BF16)
    ct_sc[h] = c["decay"][:, h:h + 1] * state + _dot(kw_t, v_ext)
    between()
    big_m = c["big_m"][:, h:h + 1]
    decay_mat = jnp.exp(jnp.where(c["causal"], (a_row + math.log(scale)) - big_m, -jnp.inf))
    p = (qk * decay_mat).astype(BF16)
    q_inter = (qh.astype(F32) * c["inter"][:, h:h + 1]).astype(BF16)
    tot = _dot(jnp.concatenate([p, q_inter], axis=1),
               jnp.concatenate([v_ext, state.astype(BF16)], axis=0))
    num = tot[:, :A_DV]
    den = tot[:, A_DV:A_DV + 1]
    inv = 1.0 / jnp.maximum(jnp.abs(den), c["floor"][:, h:h + 1])
    ms = jnp.mean(num * num, axis=1, keepdims=True)
    fac = inv * lax.rsqrt(inv * inv * ms + EPS)
    return (num * fac * gain * og.astype(F32)).astype(BF16)


A_PROJ_W = A_COL_XQ
A_UNIT_W = MXU_DIM


def _mlstm_layer_kernel(blocks_per_row, x_ref, g_ref, w_ref, wg_ref, gb_ref, gq_ref, gk_ref, hg_ref,
                        xq_ref, y_ref, proj_sc, rec_sc, ct_sc, m_sc):
    s = pl.program_id(0)

    @pl.when(s == 0)
    def _():
        proj_sc[...] = jnp.zeros_like(proj_sc)
        rec_sc[...] = jnp.zeros_like(rec_sc)

    @pl.when(lax.rem(jnp.maximum(s - 1, 0), blocks_per_row) == 0)
    def _():
        ct_sc[...] = jnp.zeros_like(ct_sc)
        m_sc[...] = jnp.zeros_like(m_sc)

    put = proj_sc.at[lax.rem(s, 2)]
    put_rec = rec_sc.at[lax.rem(s, 2)]
    get = proj_sc.at[lax.rem(s + 1, 2)]
    get_rec = rec_sc.at[lax.rem(s + 1, 2)]

    xn = _rms(x_ref[...], g_ref[...]).astype(BF16)

    def project(c):
        cols = slice(c * A_UNIT_W, (c + 1) * A_UNIT_W)
        y = _dot(xn, w_ref[:, cols])
        if cols.start >= A_COL_XQ:
            _store_mem_queries(y, gq_ref, gk_ref, xq_ref, cols.start - A_COL_XQ)
            return
        if cols.start >= A_COL_O:
            y = jax.nn.sigmoid(y)
        put[:, cols] = y.astype(BF16)

    chunks = [slice(c * MLSTM_CHUNK, (c + 1) * MLSTM_CHUNK) for c in range(ROW_TILE // MLSTM_CHUNK)]
    setups = {}

    def recur(c, h, between):
        rows = chunks[c]
        if h == 0:
            setups[c] = _mlstm_chunk_setup(get_rec[rows, :], get[rows, A_COL_K:A_COL_K + A_QK], m_sc)
        qk = slice(h * A_DQK, (h + 1) * A_DQK)
        v = slice(h * A_DV, (h + 1) * A_DV)
        y_ref[rows, v] = _mlstm_head(
            h, setups[c], get[rows, A_COL_Q + qk.start:A_COL_Q + qk.stop],
            get[rows, A_COL_V + v.start:A_COL_V + v.stop],
            get[rows, A_COL_O + v.start:A_COL_O + v.stop], hg_ref[:, v], ct_sc, between)

    gates = _dot(xn, wg_ref[...]) + gb_ref[...]
    project(0)
    project(1)
    for rows in chunks:
        put_rec[rows, :] = _mlstm_gate_record(gates[rows])
    todo = list(range(2, A_MAIN // A_UNIT_W))
    heads = [(c, h) for c in range(len(chunks)) for h in range(A_HEADS)]
    base, extra = divmod(len(todo), len(heads))
    for i, (c, h) in enumerate(heads):
        share = [todo.pop(0) for _ in range(base + (i < extra))]
        recur(c, h, lambda share=share: [project(u) for u in share])


def _mlstm_layer(x, mix_g, w_main, w_gates, gate_b, gq, gk, h_gain, layer, j, bsz, seq):
    n = bsz * seq
    nb = n // ROW_TILE
    last = nb - 1
    return pl.pallas_call(
        functools.partial(_mlstm_layer_kernel, seq // ROW_TILE),
        out_shape=(jax.ShapeDtypeStruct((n, M_Q), BF16),
                   jax.ShapeDtypeStruct((n, A_V), BF16)),
        grid=(nb + 1,),
        in_specs=[
            pl.BlockSpec((ROW_TILE, D_MODEL), lambda s: (jnp.minimum(s, last), 0)),
            _resident((None, 1, D_MODEL), lambda s: (layer, 0, 0)),
            _resident((None, D_MODEL, A_MAIN), lambda s: (j, 0, 0)),
            _resident((None, D_MODEL, LANES), lambda s: (j, 0, 0)),
            _resident((None, 1, LANES), lambda s: (j, 0, 0)),
            _resident((None, 1, M_HD), lambda s: (layer, 0, 0)),
            _resident((None, 1, M_HD), lambda s: (layer, 0, 0)),
            _resident((None, 1, A_V), lambda s: (j, 0, 0)),
        ],
        out_specs=(pl.BlockSpec((ROW_TILE, M_Q), lambda s: (jnp.minimum(s, last), 0)),
                   pl.BlockSpec((ROW_TILE, A_V), lambda s: (jnp.maximum(s - 1, 0), 0))),
        scratch_shapes=[pltpu.VMEM((2, ROW_TILE, A_PROJ_W), BF16),
                        pltpu.VMEM((2, ROW_TILE, LANES), F32),
                        pltpu.VMEM((A_HEADS, A_DQK, A_STATE_W), F32),
                        pltpu.VMEM((SUBLANES, LANES), F32)],
        compiler_params=_params("arbitrary"),
        name="mlstm_layer",
    )(x, mix_g, w_main, w_gates, gate_b, gq, gk, h_gain)


SWA_SUB = B_HD
SWA_SPAN = B_WINDOW + SWA_SUB


def _swa_units(sink_ref, q_ref, kv, first_block, y_ref):
    sub_q, span_k = SWA_SUB, SWA_SPAN
    pairs = B_GROUP // 2
    keys = kv[:, :B_KV]
    vals = kv[:, B_KV:]
    lane_kv = lax.broadcasted_iota(jnp.int32, keys.shape, 1)
    lane_q = lax.broadcasted_iota(jnp.int32, (sub_q, LANES), 1)
    kk = lax.broadcasted_iota(jnp.int32, (span_k, LANES), 0)
    qq = lax.broadcasted_iota(jnp.int32, (span_k, LANES), 1) % sub_q
    dist = B_WINDOW + qq - kk
    in_window = (dist >= 0) & (dist < B_WINDOW)
    odd_lane = lax.broadcasted_iota(jnp.int32, (1, LANES), 1) >= sub_q

    def both_halves(x, grp):
        own = jnp.where((lane_kv // B_HD) == grp, x, 0.0)
        return own + pltpu.roll(own, B_HD, axis=1)

    kg = [both_halves(keys, grp).astype(BF16) for grp in range(B_KV_HEADS)]
    vg_t = [both_halves(vals, grp).T.astype(BF16) for grp in range(B_KV_HEADS)]
    sink_rows = [jnp.where(odd_lane, sink_ref[2 * p + 1], sink_ref[2 * p]) * LOG2E for p in range(B_HEADS // 2)]

    def scores_of(grp, t):
        slabs = []
        for pp in range(pairs):
            c0 = (grp * pairs + pp) * LANES
            slab = q_ref[t * sub_q:(t + 1) * sub_q, c0:c0 + LANES]
            zero = jnp.zeros_like(slab)
            slabs.append(jnp.where(lane_q < B_HD, slab, zero))
            slabs.append(jnp.where(lane_q >= B_HD, slab, zero))
        return _dot_nt(kg[grp][t * sub_q:t * sub_q + span_k], jnp.concatenate(slabs, axis=0))

    units = [(grp, t) for grp in range(B_KV_HEADS) for t in range(ROW_TILE // sub_q)]
    ahead = [scores_of(*units[0])]

    def unit(u, between):
        grp, t = units[u]
        scores_t = ahead[0]
        if u + 1 < len(units):
            ahead[0] = scores_of(*units[u + 1])
        between()
        rows = slice(t * sub_q, (t + 1) * sub_q)
        span = slice(t * sub_q, t * sub_q + span_k)
        valid = in_window
        if t * sub_q < B_WINDOW:
            valid = valid & ((kk >= B_WINDOW - t * sub_q) | jnp.logical_not(first_block))
        probs = []
        inv = []
        for pp in range(pairs):
            sink = sink_rows[grp * pairs + pp]
            s = jnp.where(valid, scores_t[:, pp * LANES:(pp + 1) * LANES], -jnp.inf)
            m = jnp.maximum(jnp.max(s, axis=0, keepdims=True), sink)
            p = jnp.exp2(s - m)
            inv.append(1.0 / (jnp.sum(p, axis=0, keepdims=True) + jnp.exp2(sink - m)))
            probs.append(p.astype(BF16))
        out_t = _dot(vg_t[grp][:, span], jnp.concatenate(probs, axis=1))
        for pp in range(pairs):
            o = (out_t[:, pp * LANES:(pp + 1) * LANES] * inv[pp]).T
            c0 = (grp * pairs + pp) * LANES
            y_ref[rows, c0:c0 + LANES] = jnp.where(lane_q < B_HD, o[:sub_q], o[sub_q:]).astype(BF16)

    return [functools.partial(unit, u) for u in range(len(units))]


B_PROJ_W = B_Q + 2 * B_KV


def _swa_layer_kernel(blocks_per_row, sink_ref, x_ref, pos_ref, g_ref, w_ref, gqk_ref, tab_ref, gq_ref, gk_ref,
                      xq_ref, y_ref, proj_sc):
    s = pl.program_id(0)

    @pl.when(s == 0)
    def _():
        proj_sc[...] = jnp.zeros_like(proj_sc)

    put = proj_sc.at[lax.rem(s, 3)]
    get = proj_sc.at[lax.rem(s + 2, 3)]
    before = proj_sc.at[lax.rem(s + 1, 3)]
    first_block = lax.rem(jnp.maximum(s - 1, 0), blocks_per_row) == 0

    kv = jnp.concatenate([before[ROW_TILE - B_WINDOW:, B_Q:B_PROJ_W], get[:, B_Q:B_PROJ_W]], axis=0).astype(F32)
    attend = _swa_units(sink_ref, get, kv, first_block, y_ref)

    xn = _rms(x_ref[...], g_ref[...]).astype(BF16)
    norm_rope, tables = _rope_projector(pos_ref, gqk_ref, tab_ref, put)
    cos_q, sin_q, cos_k, sin_k = tables
    n_q = B_Q // MXU_DIM
    cols = [s_ * MXU_DIM for s_ in range(n_q)] + [B_COL_KV, B_COL_XQ, B_COL_XQ + MXU_DIM]
    ahead = [_dot(xn, w_ref[:, cols[0]:cols[0] + MXU_DIM])]

    def project(i):
        slab = ahead[0]
        if i + 1 < len(cols):
            ahead[0] = _dot(xn, w_ref[:, cols[i + 1]:cols[i + 1] + MXU_DIM])
        if i < n_q:
            norm_rope(slab, cos_q, sin_q, i * MXU_DIM)
        elif i == n_q:
            v = slab[:, B_KV:]
            norm_rope(slab, cos_k, sin_k, B_Q)
            put[:, B_Q + B_KV:B_PROJ_W] = v.astype(BF16)
        else:
            _store_mem_queries(slab, gq_ref, gk_ref, xq_ref, cols[i] - B_COL_XQ)

    todo = list(range(len(cols)))
    every = len(attend) // len(todo)
    for u, unit in enumerate(attend):
        share = [todo.pop(0)] if (u % every == 0 and todo) else []
        unit(lambda share=share: [project(i) for i in share])


def _swa_layer(x, pos, sinks, mix_g, w_main, gqk, rope_tab, gq, gk, layer, j, bsz, seq):
    n = bsz * seq
    nb = n // ROW_TILE
    last = nb - 1
    return pl.pallas_call(
        functools.partial(_swa_layer_kernel, seq // ROW_TILE),
        out_shape=(jax.ShapeDtypeStruct((n, M_Q), BF16),
                   jax.ShapeDtypeStruct((n, B_Q), BF16)),
        grid=(nb + 1,),
        in_specs=[
            pl.BlockSpec(memory_space=pltpu.SMEM),
            pl.BlockSpec((ROW_TILE, D_MODEL), lambda s: (jnp.minimum(s, last), 0)),
            pl.BlockSpec((None, 1, ROW_TILE), lambda s: (jnp.minimum(s, last), 0, 0)),
            _resident((None, 1, D_MODEL), lambda s: (layer, 0, 0)),
            _resident((None, D_MODEL, B_MAIN), lambda s: (j, 0, 0)),
            _resident((None, 4, LANES), lambda s: (j, 0, 0)),
            _resident((ROT_DIM, 2 * LANES), lambda s: (0, 0)),
            _resident((None, 1, M_HD), lambda s: (layer, 0, 0)),
            _resident((None, 1, M_HD), lambda s: (layer, 0, 0)),
        ],
        out_specs=(pl.BlockSpec((ROW_TILE, M_Q), lambda s: (jnp.minimum(s, last), 0)),
                   pl.BlockSpec((ROW_TILE, B_Q), lambda s: (jnp.maximum(s - 1, 0), 0))),
        scratch_shapes=[pltpu.VMEM((3, ROW_TILE, B_PROJ_W), BF16)],
        compiler_params=_params("arbitrary"),
        name="swa_layer",
    )(sinks, x, pos, mix_g, w_main, gqk, rope_tab, gq, gk)


def _outproj_kernel(x_ref, yt_ref, xq_ref, mkt_ref, mv_ref, w_ref, o_ref):
    y_dim = yt_ref.shape[1]
    head_cols = [slice(h * M_HD, (h + 1) * M_HD) for h in range(M_HEADS)]
    scores = [_dot(xq_ref[:, sl], mkt_ref[sl, :]) for sl in head_cols]
    acc = x_ref[...] + _dot(yt_ref[...], w_ref[:y_dim, :])
    heads = []
    for s, sl in zip(scores, head_cols):
        p = jnp.exp(s - jnp.max(s, axis=1, keepdims=True))
        inv = 1.0 / jnp.sum(p, axis=1, keepdims=True)
        heads.append((_dot(p.astype(BF16), mv_ref[:, sl]) * inv).astype(BF16))
    o_ref[...] = acc + _dot(jnp.concatenate(heads, axis=1), w_ref[y_dim:, :])


def _outproj(x, y_tok, xq, mem_k, mem_v, w_out, j, bsz, seq):
    nt = seq // ROW_TILE
    y_dim = y_tok.shape[1]
    return pl.pallas_call(
        _outproj_kernel,
        out_shape=jax.ShapeDtypeStruct((bsz * seq, D_MODEL), F32),
        grid=(bsz, nt),
        in_specs=[
            pl.BlockSpec((ROW_TILE, D_MODEL), lambda b, t: (b * nt + t, 0)),
            pl.BlockSpec((ROW_TILE, y_dim), lambda b, t: (b * nt + t, 0)),
            pl.BlockSpec((ROW_TILE, M_Q), lambda b, t: (b * nt + t, 0)),
            pl.BlockSpec((None, M_Q, N_MEM), lambda b, t: (b, 0, 0)),
            pl.BlockSpec((None, N_MEM, M_Q), lambda b, t: (b, 0, 0)),
            _resident((None, y_dim + M_Q, D_MODEL), lambda b, t: (j, 0, 0)),
        ],
        out_specs=pl.BlockSpec((ROW_TILE, D_MODEL), lambda b, t: (b * nt + t, 0)),
        compiler_params=_params("parallel", "parallel"),
        name="outproj",
    )(x, y_tok, xq, mem_k, mem_v, w_out)


def _rope_row_tables():
    inv_freq = ROPE_THETA ** (-np.arange(0, ROT_DIM, 2, dtype=np.float32) / ROT_DIM)
    tab = np.zeros((ROT_DIM, 2 * LANES), np.float32)
    tab[:, :LANES] = np.concatenate([inv_freq, inv_freq])[:, None]
    tab[:, LANES:] = np.where(np.arange(ROT_DIM) < ROT_HALF, -1.0, 1.0)[:, None]
    return jnp.asarray(tab)


def _rope_gain_rows(gain):
    d = np.arange(B_HD)
    partner = np.where(d < ROT_HALF, d + ROT_HALF, np.where(d < ROT_DIM, d - ROT_HALF, d))
    reps = LANES // B_HD
    return jnp.stack([jnp.tile(gain, (1, reps)), jnp.tile(gain[:, partner], (1, reps))], axis=1)


def kernel(x, mem, positions, mem_norm_g, mem_w_kv, ffn1_norm_g, ffn1_w_in, ffn1_w_out,
           mix_norm_g, ffn2_norm_g, ffn2_w_in, ffn2_w_out, xa_q_norm_g, xa_k_norm_g,
           a_w_in, a_gate_b, a_h_norm_g, a_w_out,
           b_w_in, b_q_norm_g, b_k_norm_g, b_sinks, b_w_out):
    bsz, seq, _ = x.shape
    n = bsz * seq
    assert seq % ROW_TILE == 0 and n % FFN_ROWS == 0 and ROW_TILE % MLSTM_CHUNK == 0 and ROW_TILE % SWA_SUB == 0

    bf = lambda w: w.astype(BF16)
    row3 = lambda g: g.reshape(g.shape[0], 1, g.shape[1])
    ffn1_in, ffn1_out, ffn2_in, ffn2_out = bf(ffn1_w_in), bf(ffn1_w_out), bf(ffn2_w_in), bf(ffn2_w_out)
    a_main = bf(jnp.concatenate([a_w_in[..., :A_COL_XQ], a_w_in[..., A_TOK:]], axis=-1))
    a_gw = bf(jnp.pad(a_w_in[..., A_COL_XQ:A_TOK], ((0, 0), (0, 0), (0, LANES - A_GATES))))
    a_gb = row3(jnp.pad(a_gate_b, ((0, 0), (0, LANES - A_GATES))))
    b_main = bf(jnp.concatenate([b_w_in[..., :B_Q], b_w_in[..., B_TOK:], b_w_in[..., B_Q:B_TOK]], axis=-1))
    b_gqk = jnp.concatenate([_rope_gain_rows(b_q_norm_g), _rope_gain_rows(b_k_norm_g)], axis=1)
    a_out, b_out = bf(a_w_out), bf(b_w_out)
    rope_tab = _rope_row_tables()
    pos = positions.reshape(n // ROW_TILE, 1, ROW_TILE)
    g_ffn1, g_mix, g_ffn2 = row3(ffn1_norm_g), row3(mix_norm_g), row3(ffn2_norm_g)
    g_xq, g_xk, g_h = row3(xa_q_norm_g), row3(xa_k_norm_g), row3(a_h_norm_g)

    mem_k, mem_v = _memkv(mem, mem_norm_g.reshape(1, D_MODEL), bf(mem_w_kv))
    xf = x.reshape(n, D_MODEL)
    for i in range(DEPTH):
        j = i // 2
        xf = _ffn(xf, g_ffn1, ffn1_in, ffn1_out, i)
        if i % 2 == 0:
            xq, y_tok = _mlstm_layer(xf, g_mix, a_main, a_gw, a_gb, g_xq, g_xk, g_h, i, j, bsz, seq)
            xf = _outproj(xf, y_tok, xq, mem_k, mem_v, a_out, j, bsz, seq)
        else:
            xq, y_tok = _swa_layer(xf, pos, b_sinks[j], g_mix, b_main, b_gqk, rope_tab, g_xq, g_xk, i, j, bsz, seq)
            xf = _outproj(xf, y_tok, xq, mem_k, mem_v, b_out, j, bsz, seq)
        xf = _ffn(xf, g_ffn2, ffn2_in, ffn2_out, i)
    return xf.reshape(bsz, seq, D_MODEL)
```

```python
import functools
import math

import numpy as np
import jax
import jax.numpy as jnp
from jax import lax
from jax.experimental import pallas as pl
from jax.experimental.pallas import tpu as pltpu

F32 = jnp.float32
BF16 = jnp.bfloat16

D_MODEL = 1024
DEPTH = 4
EPS = 1e-6
D_FF = 2816

A_HEADS = 4
A_DQK = 128
A_DV = 256
A_QK = A_HEADS * A_DQK
A_V = A_HEADS * A_DV
A_GATES = 2 * A_HEADS
A_TOK = 2 * A_QK + 2 * A_V + A_GATES

B_HEADS = 16
B_KV_HEADS = 2
B_GROUP = B_HEADS // B_KV_HEADS
B_HD = 64
B_WINDOW = 128
B_Q = B_HEADS * B_HD
B_KV = B_KV_HEADS * B_HD
B_TOK = B_Q + 2 * B_KV
ROT_DIM = 16
ROT_HALF = ROT_DIM // 2
ROPE_THETA = 500000.0

N_MEM = 256
M_HEADS = 4
M_HD = 128
M_Q = M_HEADS * M_HD

LOG2E = math.log2(math.e)

LANES = 128
SUBLANES = 8
MXU_DIM = 256
VMEM_LIMIT = 56 * 1024 * 1024

ROW_TILE = 512
FF_TILE = MXU_DIM
FFN_ROWS = 1024
FFN_PART = 512
MLSTM_CHUNK = 256

A_COL_Q, A_COL_K, A_COL_V, A_COL_O = 0, A_QK, 2 * A_QK, 2 * A_QK + A_V
A_COL_XQ = 2 * A_QK + 2 * A_V
A_MAIN = A_COL_XQ + M_Q
G_LANE_A, G_LANE_B, G_LANE_CM = 0, A_HEADS, 2 * A_HEADS
B_COL_XQ = B_Q
B_COL_KV = B_Q + M_Q
B_MAIN = B_COL_KV + 2 * B_KV


def _params(*semantics):
    return pltpu.CompilerParams(dimension_semantics=semantics, vmem_limit_bytes=VMEM_LIMIT)


def _resident(block_shape, index_map):
    return pl.BlockSpec(block_shape, index_map, pipeline_mode=pl.Buffered(1))


def _rms(x, gain):
    ms = jnp.mean(x * x, axis=-1, keepdims=True)
    return x * lax.rsqrt(ms + EPS) * gain


def _dot(a, b):
    return jnp.dot(a, b, preferred_element_type=F32)


def _dot_nt(a, b):
    return lax.dot_general(a, b, (((1,), (1,)), ((), ())), preferred_element_type=F32)


def _split2(x):
    hi = x.astype(BF16)
    return hi, (x - hi.astype(F32)).astype(BF16)


def _ffn_kernel(x_ref, g_ref, win_ref, wout_ref, o_ref):
    parts = [slice(r * FFN_PART, (r + 1) * FFN_PART) for r in range(FFN_ROWS // FFN_PART)]
    x = [x_ref[rows, :] for rows in parts]
    xn = [_rms(xr, g_ref[...]).astype(BF16) for xr in x]
    acc = [jnp.zeros_like(xr) for xr in x]
    for j in range(D_FF // FF_TILE):
        lo = j * FF_TILE
        for r in range(len(parts)):
            gate = _dot(xn[r], win_ref[:, lo:lo + FF_TILE])
            up = _dot(xn[r], win_ref[:, D_FF + lo:D_FF + lo + FF_TILE])
            h = (gate * jax.nn.sigmoid(gate) * up).astype(BF16)
            acc[r] = acc[r] + _dot(h, wout_ref[lo:lo + FF_TILE, :])
    for r, rows in enumerate(parts):
        o_ref[rows, :] = x[r] + 0.5 * acc[r]


def _ffn_stream(x, g_ref, win_ref, wout_ref):
    xn = _rms(x, g_ref[...]).astype(BF16)
    acc = [jnp.zeros_like(x)]

    def unit(j):
        lo = j * FF_TILE
        gate = _dot(xn, win_ref[:, lo:lo + FF_TILE])
        up = _dot(xn, win_ref[:, D_FF + lo:D_FF + lo + FF_TILE])
        h = (gate * jax.nn.sigmoid(gate) * up).astype(BF16)
        acc[0] = acc[0] + _dot(h, wout_ref[lo:lo + FF_TILE, :])

    return [functools.partial(unit, j) for j in range(D_FF // FF_TILE)], lambda: x + 0.5 * acc[0]


def _shares(n_items, n_takers):
    base, extra = divmod(n_items, n_takers)
    return [base + (i < extra) for i in range(n_takers)]


def _ffn(x, gains, w_in, w_out, layer):
    n = x.shape[0]
    return pl.pallas_call(
        _ffn_kernel,
        out_shape=jax.ShapeDtypeStruct((n, D_MODEL), F32),
        grid=(n // FFN_ROWS,),
        in_specs=[
            pl.BlockSpec((FFN_ROWS, D_MODEL), lambda i: (i, 0)),
            _resident((None, 1, D_MODEL), lambda i: (layer, 0, 0)),
            _resident((None, D_MODEL, 2 * D_FF), lambda i: (layer, 0, 0)),
            _resident((None, D_FF, D_MODEL), lambda i: (layer, 0, 0)),
        ],
        out_specs=pl.BlockSpec((FFN_ROWS, D_MODEL), lambda i: (i, 0)),
        compiler_params=_params("parallel"),
        name="ffn",
    )(x, gains, w_in, w_out)


def _memkv_kernel(mem_ref, g_ref, w_ref, k_ref, v_ref):
    mn = _rms(mem_ref[...], g_ref[...]).astype(BF16)
    kv = _dot(mn, w_ref[...])
    for h in range(M_HEADS):
        kh = kv[:, h * M_HD:(h + 1) * M_HD]
        ms = jnp.mean(kh * kh, axis=-1, keepdims=True)
        k_ref[h * M_HD:(h + 1) * M_HD, :] = (kh * lax.rsqrt(ms + EPS)).T.astype(BF16)
    v_ref[...] = kv[:, M_Q:].astype(BF16)


def _memkv(mem, gain, w_kv):
    bsz = mem.shape[0]
    return pl.pallas_call(
        _memkv_kernel,
        out_shape=(jax.ShapeDtypeStruct((bsz, M_Q, N_MEM), BF16),
                   jax.ShapeDtypeStruct((bsz, N_MEM, M_Q), BF16)),
        grid=(bsz,),
        in_specs=[
            pl.BlockSpec((None, N_MEM, D_MODEL), lambda b: (b, 0, 0)),
            _resident((1, D_MODEL), lambda b: (0, 0)),
            _resident((D_MODEL, 2 * M_Q), lambda b: (0, 0)),
        ],
        out_specs=(pl.BlockSpec((None, M_Q, N_MEM), lambda b: (b, 0, 0)),
                   pl.BlockSpec((None, N_MEM, M_Q), lambda b: (b, 0, 0))),
        compiler_params=_params("parallel"),
        name="memkv",
    )(mem, gain, w_kv)


def _store_mem_queries(xq, gq_ref, gk_ref, out_ref, col0):
    geff = gq_ref[...] * gk_ref[...] * (M_HD ** -0.5)
    for h in range(xq.shape[1] // M_HD):
        s = xq[:, h * M_HD:(h + 1) * M_HD]
        ms = jnp.mean(s * s, axis=-1, keepdims=True)
        out_ref[:, col0 + h * M_HD:col0 + (h + 1) * M_HD] = (s * lax.rsqrt(ms + EPS) * geff).astype(BF16)


def _log_sigmoid(x):
    return jnp.minimum(x, 0.0) - jnp.log(1.0 + jnp.exp(-jnp.abs(x)))


def _causal_mask(n):
    row = lax.broadcasted_iota(jnp.int32, (n, n), 0)
    col = lax.broadcasted_iota(jnp.int32, (n, n), 1)
    return col <= row


def _mlstm_gate_record(g):
    chunk = g.shape[0]
    causal = _causal_mask(chunk)
    hi, lo = _split2(_log_sigmoid(g))
    both = _dot(causal.astype(BF16), jnp.concatenate([hi, lo], axis=1))
    b = both[:, :LANES] + both[:, LANES:]
    a = g - pltpu.roll(b, LANES - A_HEADS, axis=1)
    a_t = a.T
    lane = lax.broadcasted_iota(jnp.int32, (chunk, LANES), 1)
    rec = jnp.where(lane < G_LANE_B, a, jnp.where(lane < G_LANE_CM, b, 0.0))
    for h in range(A_HEADS):
        run_max = jnp.max(jnp.where(causal, a_t[h:h + 1, :], -jnp.inf), axis=1, keepdims=True)
        rec = jnp.where(lane == G_LANE_CM + h, run_max, rec)
    return rec


ROPE_PAD = B_HD - ROT_DIM


def _rope_projector(pos_ref, gqk_ref, tab_ref, out_ref):
    ang = tab_ref[:, 0:1] * pos_ref[...].astype(F32)
    cos_c = jnp.cos(ang)
    sin_c = jnp.sin(ang) * tab_ref[:, LANES:LANES + 1]
    pad = jnp.zeros((ROPE_PAD, ang.shape[1]), F32)
    cos_t = jnp.concatenate([cos_c, pad, cos_c, pad], axis=0).T
    sin_t = jnp.concatenate([sin_c, pad, sin_c, pad], axis=0).T
    lane = lax.broadcasted_iota(jnp.int32, (1, LANES), 1) % B_HD
    cos_t = cos_t + (lane >= ROT_DIM).astype(F32)
    first_half = lane < ROT_HALF
    r = lax.broadcasted_iota(jnp.int32, (MXU_DIM, MXU_DIM), 0) // B_HD
    c = lax.broadcasted_iota(jnp.int32, (MXU_DIM, MXU_DIM), 1) // B_HD
    seg = jnp.where(r == c, 1.0 / B_HD, 0.0).astype(BF16)

    def norm_rope(slab, cos_g, sin_g, col0):
        hi, lo = _split2(slab * slab)
        inv = lax.rsqrt(_dot(hi, seg) + _dot(lo, seg) + EPS)
        for half in range(MXU_DIM // LANES):
            sl = slice(half * LANES, (half + 1) * LANES)
            xh = slab[:, sl]
            partner = jnp.where(first_half, pltpu.roll(xh, LANES - ROT_HALF, axis=1),
                                pltpu.roll(xh, ROT_HALF, axis=1))
            out = (xh * cos_g + partner * sin_g) * inv[:, sl]
            out_ref[:, col0 + half * LANES:col0 + (half + 1) * LANES] = out.astype(BF16)

    q_scale = B_HD ** -0.5 * LOG2E
    cos_q, sin_q = cos_t * (gqk_ref[0:1, :] * q_scale), sin_t * (gqk_ref[1:2, :] * q_scale)
    cos_k, sin_k = cos_t * gqk_ref[2:3, :], sin_t * gqk_ref[3:4, :]
    return norm_rope, (cos_q, sin_q, cos_k, sin_k)


A_STATE_W = A_DV + LANES


def _mlstm_chunk_setup(rec, k, m_sc):
    chunk = rec.shape[0]
    scale = A_DQK ** -0.5
    b = pltpu.roll(rec, LANES - G_LANE_B, axis=1)
    run_max = pltpu.roll(rec, LANES - G_LANE_CM, axis=1)
    m_row = m_sc[0:1, :]
    big_m = jnp.maximum(run_max, m_row)
    m_last = big_m[chunk - 1:chunk, :]
    m_sc[0:1, :] = b[chunk - 1:chunk, :] + m_last
    return dict(
        a_t=rec.T,
        big_m=big_m,
        inter=scale * jnp.exp(m_row - big_m),
        floor=jnp.exp(-(b + big_m)),
        m_last=m_last,
        decay=jnp.exp(m_row - m_last),
        k_t=k.astype(F32).T,
        causal=_causal_mask(chunk),
        ones_col=(lax.broadcasted_iota(jnp.int32, (chunk, LANES), 1) == 0).astype(BF16),
    )


def _mlstm_head(h, c, qh, vh, og, gain, ct_sc, between):
    scale = A_DQK ** -0.5
    a_row = c["a_t"][h:h + 1, :]
    k_t = c["k_t"][h * A_DQK:(h + 1) * A_DQK, :]
    qk = _dot(qh, k_t.astype(BF16))
    v_ext = jnp.concatenate([vh, c["ones_col"]], axis=1)
    state = ct_sc[h]
    w_row = jnp.exp(a_row - c["m_last"][:, h:h + 1])
    kw_t = (k_t * w_row).astype(BF16)
    ct_sc[h] = c["decay"][:, h:h + 1] * state + _dot(kw_t, v_ext)
    between()
    big_m = c["big_m"][:, h:h + 1]
    decay_mat = jnp.exp(jnp.where(c["causal"], (a_row + math.log(scale)) - big_m, -jnp.inf))
    p = (qk * decay_mat).astype(BF16)
    q_inter = (qh.astype(F32) * c["inter"][:, h:h + 1]).astype(BF16)
    tot = _dot(jnp.concatenate([p, q_inter], axis=1),
               jnp.concatenate([v_ext, state.astype(BF16)], axis=0))
    num = tot[:, :A_DV]
    den = tot[:, A_DV:A_DV + 1]
    inv = 1.0 / jnp.maximum(jnp.abs(den), c["floor"][:, h:h + 1])
    ms = jnp.mean(num * num, axis=1, keepdims=True)
    fac = inv * lax.rsqrt(inv * inv * ms + EPS)
    return (num * fac * gain * og.astype(F32)).astype(BF16)


A_PROJ_W = A_COL_XQ
A_UNIT_W = MXU_DIM


def _mlstm_layer_kernel(blocks_per_row, x_ref, g1_ref, win_ref, wout_ref, g_ref, w_ref, wg_ref, gb_ref,
                        gq_ref, gk_ref, hg_ref, x1_ref, xq_ref, y_ref, proj_sc, rec_sc, ct_sc, m_sc):
    s = pl.program_id(0)

    @pl.when(s == 0)
    def _():
        proj_sc[...] = jnp.zeros_like(proj_sc)
        rec_sc[...] = jnp.zeros_like(rec_sc)

    @pl.when(lax.rem(jnp.maximum(s - 1, 0), blocks_per_row) == 0)
    def _():
        ct_sc[...] = jnp.zeros_like(ct_sc)
        m_sc[...] = jnp.zeros_like(m_sc)

    put = proj_sc.at[lax.rem(s, 2)]
    put_rec = rec_sc.at[lax.rem(s, 2)]
    get = proj_sc.at[lax.rem(s + 1, 2)]
    get_rec = rec_sc.at[lax.rem(s + 1, 2)]

    ffn_units, ffn_result = _ffn_stream(x_ref[...], g1_ref, win_ref, wout_ref)
    xn = []

    def project(c):
        cols = slice(c * A_UNIT_W, (c + 1) * A_UNIT_W)
        y = _dot(xn[0], w_ref[:, cols])
        if cols.start >= A_COL_XQ:
            _store_mem_queries(y, gq_ref, gk_ref, xq_ref, cols.start - A_COL_XQ)
            return
        if cols.start >= A_COL_O:
            y = jax.nn.sigmoid(y)
        put[:, cols] = y.astype(BF16)

    chunks = [slice(c * MLSTM_CHUNK, (c + 1) * MLSTM_CHUNK) for c in range(ROW_TILE // MLSTM_CHUNK)]
    setups = {}

    def recur(c, h, between):
        rows = chunks[c]
        if h == 0:
            setups[c] = _mlstm_chunk_setup(get_rec[rows, :], get[rows, A_COL_K:A_COL_K + A_QK], m_sc)
        qk = slice(h * A_DQK, (h + 1) * A_DQK)
        v = slice(h * A_DV, (h + 1) * A_DV)
        y_ref[rows, v] = _mlstm_head(
            h, setups[c], get[rows, A_COL_Q + qk.start:A_COL_Q + qk.stop],
            get[rows, A_COL_V + v.start:A_COL_V + v.stop],
            get[rows, A_COL_O + v.start:A_COL_O + v.stop], hg_ref[:, v], ct_sc, between)

    def start_projection():
        x1 = ffn_result()
        x1_ref[...] = x1
        xn.append(_rms(x1, g_ref[...]).astype(BF16))
        gates = _dot(xn[0], wg_ref[...]) + gb_ref[...]
        project(0)
        project(1)
        for rows in chunks:
            put_rec[rows, :] = _mlstm_gate_record(gates[rows])

    stream = ffn_units + [start_projection] + [functools.partial(project, u) for u in range(2, A_MAIN // A_UNIT_W)]
    heads = [(c, h) for c in range(len(chunks)) for h in range(A_HEADS)]
    for (c, h), size in zip(heads, _shares(len(stream), len(heads))):
        share, stream = stream[:size], stream[size:]
        recur(c, h, lambda share=share: [unit() for unit in share])


def _mlstm_layer(x, ffn_g, ffn_in, ffn_out, mix_g, w_main, w_gates, gate_b, gq, gk, h_gain, layer, j, bsz, seq):
    n = bsz * seq
    nb = n // ROW_TILE
    last = nb - 1
    return pl.pallas_call(
        functools.partial(_mlstm_layer_kernel, seq // ROW_TILE),
        out_shape=(jax.ShapeDtypeStruct((n, D_MODEL), F32),
                   jax.ShapeDtypeStruct((n, M_Q), BF16),
                   jax.ShapeDtypeStruct((n, A_V), BF16)),
        grid=(nb + 1,),
        in_specs=[
            pl.BlockSpec((ROW_TILE, D_MODEL), lambda s: (jnp.minimum(s, last), 0)),
            _resident((None, 1, D_MODEL), lambda s: (layer, 0, 0)),
            _resident((None, D_MODEL, 2 * D_FF), lambda s: (layer, 0, 0)),
            _resident((None, D_FF, D_MODEL), lambda s: (layer, 0, 0)),
            _resident((None, 1, D_MODEL), lambda s: (layer, 0, 0)),
            _resident((None, D_MODEL, A_MAIN), lambda s: (j, 0, 0)),
            _resident((None, D_MODEL, LANES), lambda s: (j, 0, 0)),
            _resident((None, 1, LANES), lambda s: (j, 0, 0)),
            _resident((None, 1, M_HD), lambda s: (layer, 0, 0)),
            _resident((None, 1, M_HD), lambda s: (layer, 0, 0)),
            _resident((None, 1, A_V), lambda s: (j, 0, 0)),
        ],
        out_specs=(pl.BlockSpec((ROW_TILE, D_MODEL), lambda s: (jnp.minimum(s, last), 0)),
                   pl.BlockSpec((ROW_TILE, M_Q), lambda s: (jnp.minimum(s, last), 0)),
                   pl.BlockSpec((ROW_TILE, A_V), lambda s: (jnp.maximum(s - 1, 0), 0))),
        scratch_shapes=[pltpu.VMEM((2, ROW_TILE, A_PROJ_W), BF16),
                        pltpu.VMEM((2, ROW_TILE, LANES), F32),
                        pltpu.VMEM((A_HEADS, A_DQK, A_STATE_W), F32),
                        pltpu.VMEM((SUBLANES, LANES), F32)],
        compiler_params=_params("arbitrary"),
        name="mlstm_layer",
    )(x, ffn_g, ffn_in, ffn_out, mix_g, w_main, w_gates, gate_b, gq, gk, h_gain)


SWA_SUB = B_HD
SWA_SPAN = B_WINDOW + SWA_SUB


def _swa_units(sink_ref, q_ref, kv, first_block, y_ref):
    sub_q, span_k = SWA_SUB, SWA_SPAN
    pairs = B_GROUP // 2
    keys = kv[:, :B_KV]
    vals = kv[:, B_KV:]
    lane_kv = lax.broadcasted_iota(jnp.int32, keys.shape, 1)
    lane_q = lax.broadcasted_iota(jnp.int32, (sub_q, LANES), 1)
    kk = lax.broadcasted_iota(jnp.int32, (span_k, LANES), 0)
    qq = lax.broadcasted_iota(jnp.int32, (span_k, LANES), 1) % sub_q
    dist = B_WINDOW + qq - kk
    in_window = (dist >= 0) & (dist < B_WINDOW)
    odd_lane = lax.broadcasted_iota(jnp.int32, (1, LANES), 1) >= sub_q

    def both_halves(x, grp):
        own = jnp.where((lane_kv // B_HD) == grp, x, 0.0)
        return own + pltpu.roll(own, B_HD, axis=1)

    kg = [both_halves(keys, grp).astype(BF16) for grp in range(B_KV_HEADS)]
    vg_t = [both_halves(vals, grp).T.astype(BF16) for grp in range(B_KV_HEADS)]
    sink_rows = [jnp.where(odd_lane, sink_ref[2 * p + 1], sink_ref[2 * p]) * LOG2E for p in range(B_HEADS // 2)]

    def scores_of(grp, t):
        slabs = []
        for pp in range(pairs):
            c0 = (grp * pairs + pp) * LANES
            slab = q_ref[t * sub_q:(t + 1) * sub_q, c0:c0 + LANES]
            zero = jnp.zeros_like(slab)
            slabs.append(jnp.where(lane_q < B_HD, slab, zero))
            slabs.append(jnp.where(lane_q >= B_HD, slab, zero))
        return _dot_nt(kg[grp][t * sub_q:t * sub_q + span_k], jnp.concatenate(slabs, axis=0))

    units = [(grp, t) for grp in range(B_KV_HEADS) for t in range(ROW_TILE // sub_q)]
    ahead = [scores_of(*units[0])]

    def unit(u, between):
        grp, t = units[u]
        scores_t = ahead[0]
        if u + 1 < len(units):
            ahead[0] = scores_of(*units[u + 1])
        between()
        rows = slice(t * sub_q, (t + 1) * sub_q)
        span = slice(t * sub_q, t * sub_q + span_k)
        valid = in_window
        if t * sub_q < B_WINDOW:
            valid = valid & ((kk >= B_WINDOW - t * sub_q) | jnp.logical_not(first_block))
        probs = []
        inv = []
        for pp in range(pairs):
            sink = sink_rows[grp * pairs + pp]
            s = jnp.where(valid, scores_t[:, pp * LANES:(pp + 1) * LANES], -jnp.inf)
            m = jnp.maximum(jnp.max(s, axis=0, keepdims=True), sink)
            p = jnp.exp2(s - m)
            inv.append(1.0 / (jnp.sum(p, axis=0, keepdims=True) + jnp.exp2(sink - m)))
            probs.append(p.astype(BF16))
        out_t = _dot(vg_t[grp][:, span], jnp.concatenate(probs, axis=1))
        for pp in range(pairs):
            o = (out_t[:, pp * LANES:(pp + 1) * LANES] * inv[pp]).T
            c0 = (grp * pairs + pp) * LANES
            y_ref[rows, c0:c0 + LANES] = jnp.where(lane_q < B_HD, o[:sub_q], o[sub_q:]).astype(BF16)

    return [functools.partial(unit, u) for u in range(len(units))]


B_PROJ_W = B_Q + 2 * B_KV


def _swa_layer_kernel(blocks_per_row, sink_ref, x_ref, pos_ref, g1_ref, win_ref, wout_ref, g_ref, w_ref,
                      gqk_ref, tab_ref, gq_ref, gk_ref, x1_ref, xq_ref, y_ref, proj_sc):
    s = pl.program_id(0)

    @pl.when(s == 0)
    def _():
        proj_sc[...] = jnp.zeros_like(proj_sc)

    put = proj_sc.at[lax.rem(s, 3)]
    get = proj_sc.at[lax.rem(s + 2, 3)]
    before = proj_sc.at[lax.rem(s + 1, 3)]
    first_block = lax.rem(jnp.maximum(s - 1, 0), blocks_per_row) == 0

    kv = jnp.concatenate([before[ROW_TILE - B_WINDOW:, B_Q:B_PROJ_W], get[:, B_Q:B_PROJ_W]], axis=0).astype(F32)
    attend = _swa_units(sink_ref, get, kv, first_block, y_ref)

    ffn_units, ffn_result = _ffn_stream(x_ref[...], g1_ref, win_ref, wout_ref)
    norm_rope, tables = _rope_projector(pos_ref, gqk_ref, tab_ref, put)
    cos_q, sin_q, cos_k, sin_k = tables
    n_q = B_Q // MXU_DIM
    cols = [s_ * MXU_DIM for s_ in range(n_q)] + [B_COL_KV, B_COL_XQ, B_COL_XQ + MXU_DIM]
    xn = []
    ahead = []

    def start_projection():
        x1 = ffn_result()
        x1_ref[...] = x1
        xn.append(_rms(x1, g_ref[...]).astype(BF16))
        ahead.append(_dot(xn[0], w_ref[:, cols[0]:cols[0] + MXU_DIM]))

    def project(i):
        slab = ahead[0]
        if i + 1 < len(cols):
            ahead[0] = _dot(xn[0], w_ref[:, cols[i + 1]:cols[i + 1] + MXU_DIM])
        if i < n_q:
            norm_rope(slab, cos_q, sin_q, i * MXU_DIM)
        elif i == n_q:
            v = slab[:, B_KV:]
            norm_rope(slab, cos_k, sin_k, B_Q)
            put[:, B_Q + B_KV:B_PROJ_W] = v.astype(BF16)
        else:
            _store_mem_queries(slab, gq_ref, gk_ref, xq_ref, cols[i] - B_COL_XQ)

    stream = ffn_units + [start_projection] + [functools.partial(project, i) for i in range(len(cols))]
    for unit, size in zip(attend, _shares(len(stream), len(attend))):
        share, stream = stream[:size], stream[size:]
        unit(lambda share=share: [item() for item in share])


def _swa_layer(x, pos, sinks, ffn_g, ffn_in, ffn_out, mix_g, w_main, gqk, rope_tab, gq, gk, layer, j, bsz, seq):
    n = bsz * seq
    nb = n // ROW_TILE
    last = nb - 1
    return pl.pallas_call(
        functools.partial(_swa_layer_kernel, seq // ROW_TILE),
        out_shape=(jax.ShapeDtypeStruct((n, D_MODEL), F32),
                   jax.ShapeDtypeStruct((n, M_Q), BF16),
                   jax.ShapeDtypeStruct((n, B_Q), BF16)),
        grid=(nb + 1,),
        in_specs=[
            pl.BlockSpec(memory_space=pltpu.SMEM),
            pl.BlockSpec((ROW_TILE, D_MODEL), lambda s: (jnp.minimum(s, last), 0)),
            pl.BlockSpec((None, 1, ROW_TILE), lambda s: (jnp.minimum(s, last), 0, 0)),
            _resident((None, 1, D_MODEL), lambda s: (layer, 0, 0)),
            _resident((None, D_MODEL, 2 * D_FF), lambda s: (layer, 0, 0)),
            _resident((None, D_FF, D_MODEL), lambda s: (layer, 0, 0)),
            _resident((None, 1, D_MODEL), lambda s: (layer, 0, 0)),
            _resident((None, D_MODEL, B_MAIN), lambda s: (j, 0, 0)),
            _resident((None, 4, LANES), lambda s: (j, 0, 0)),
            _resident((ROT_DIM, 2 * LANES), lambda s: (0, 0)),
            _resident((None, 1, M_HD), lambda s: (layer, 0, 0)),
            _resident((None, 1, M_HD), lambda s: (layer, 0, 0)),
        ],
        out_specs=(pl.BlockSpec((ROW_TILE, D_MODEL), lambda s: (jnp.minimum(s, last), 0)),
                   pl.BlockSpec((ROW_TILE, M_Q), lambda s: (jnp.minimum(s, last), 0)),
                   pl.BlockSpec((ROW_TILE, B_Q), lambda s: (jnp.maximum(s - 1, 0), 0))),
        scratch_shapes=[pltpu.VMEM((3, ROW_TILE, B_PROJ_W), BF16)],
        compiler_params=_params("arbitrary"),
        name="swa_layer",
    )(sinks, x, pos, ffn_g, ffn_in, ffn_out, mix_g, w_main, gqk, rope_tab, gq, gk)


def _outproj_kernel(x_ref, yt_ref, xq_ref, mkt_ref, mv_ref, w_ref, o_ref):
    y_dim = yt_ref.shape[1]
    head_cols = [slice(h * M_HD, (h + 1) * M_HD) for h in range(M_HEADS)]
    scores = [_dot(xq_ref[:, sl], mkt_ref[sl, :]) for sl in head_cols]
    acc = x_ref[...] + _dot(yt_ref[...], w_ref[:y_dim, :])
    heads = []
    for s, sl in zip(scores, head_cols):
        p = jnp.exp(s - jnp.max(s, axis=1, keepdims=True))
        inv = 1.0 / jnp.sum(p, axis=1, keepdims=True)
        heads.append((_dot(p.astype(BF16), mv_ref[:, sl]) * inv).astype(BF16))
    o_ref[...] = acc + _dot(jnp.concatenate(heads, axis=1), w_ref[y_dim:, :])


def _outproj(x, y_tok, xq, mem_k, mem_v, w_out, j, bsz, seq):
    nt = seq // ROW_TILE
    y_dim = y_tok.shape[1]
    return pl.pallas_call(
        _outproj_kernel,
        out_shape=jax.ShapeDtypeStruct((bsz * seq, D_MODEL), F32),
        grid=(bsz, nt),
        in_specs=[
            pl.BlockSpec((ROW_TILE, D_MODEL), lambda b, t: (b * nt + t, 0)),
            pl.BlockSpec((ROW_TILE, y_dim), lambda b, t: (b * nt + t, 0)),
            pl.BlockSpec((ROW_TILE, M_Q), lambda b, t: (b * nt + t, 0)),
            pl.BlockSpec((None, M_Q, N_MEM), lambda b, t: (b, 0, 0)),
            pl.BlockSpec((None, N_MEM, M_Q), lambda b, t: (b, 0, 0)),
            _resident((None, y_dim + M_Q, D_MODEL), lambda b, t: (j, 0, 0)),
        ],
        out_specs=pl.BlockSpec((ROW_TILE, D_MODEL), lambda b, t: (b * nt + t, 0)),
        compiler_params=_params("parallel", "parallel"),
        name="outproj",
    )(x, y_tok, xq, mem_k, mem_v, w_out)


def _rope_row_tables():
    inv_freq = ROPE_THETA ** (-np.arange(0, ROT_DIM, 2, dtype=np.float32) / ROT_DIM)
    tab = np.zeros((ROT_DIM, 2 * LANES), np.float32)
    tab[:, :LANES] = np.concatenate([inv_freq, inv_freq])[:, None]
    tab[:, LANES:] = np.where(np.arange(ROT_DIM) < ROT_HALF, -1.0, 1.0)[:, None]
    return jnp.asarray(tab)


def _rope_gain_rows(gain):
    d = np.arange(B_HD)
    partner = np.where(d < ROT_HALF, d + ROT_HALF, np.where(d < ROT_DIM, d - ROT_HALF, d))
    reps = LANES // B_HD
    return jnp.stack([jnp.tile(gain, (1, reps)), jnp.tile(gain[:, partner], (1, reps))], axis=1)


def kernel(x, mem, positions, mem_norm_g, mem_w_kv, ffn1_norm_g, ffn1_w_in, ffn1_w_out,
           mix_norm_g, ffn2_norm_g, ffn2_w_in, ffn2_w_out, xa_q_norm_g, xa_k_norm_g,
           a_w_in, a_gate_b, a_h_norm_g, a_w_out,
           b_w_in, b_q_norm_g, b_k_norm_g, b_sinks, b_w_out):
    bsz, seq, _ = x.shape
    n = bsz * seq
    assert seq % ROW_TILE == 0 and n % FFN_ROWS == 0 and ROW_TILE % MLSTM_CHUNK == 0 and ROW_TILE % SWA_SUB == 0

    bf = lambda w: w.astype(BF16)
    row3 = lambda g: g.reshape(g.shape[0], 1, g.shape[1])
    ffn1_in, ffn1_out, ffn2_in, ffn2_out = bf(ffn1_w_in), bf(ffn1_w_out), bf(ffn2_w_in), bf(ffn2_w_out)
    a_main = bf(jnp.concatenate([a_w_in[..., :A_COL_XQ], a_w_in[..., A_TOK:]], axis=-1))
    a_gw = bf(jnp.pad(a_w_in[..., A_COL_XQ:A_TOK], ((0, 0), (0, 0), (0, LANES - A_GATES))))
    a_gb = row3(jnp.pad(a_gate_b, ((0, 0), (0, LANES - A_GATES))))
    b_main = bf(jnp.concatenate([b_w_in[..., :B_Q], b_w_in[..., B_TOK:], b_w_in[..., B_Q:B_TOK]], axis=-1))
    b_gqk = jnp.concatenate([_rope_gain_rows(b_q_norm_g), _rope_gain_rows(b_k_norm_g)], axis=1)
    a_out, b_out = bf(a_w_out), bf(b_w_out)
    rope_tab = _rope_row_tables()
    pos = positions.reshape(n // ROW_TILE, 1, ROW_TILE)
    g_ffn1, g_mix, g_ffn2 = row3(ffn1_norm_g), row3(mix_norm_g), row3(ffn2_norm_g)
    g_xq, g_xk, g_h = row3(xa_q_norm_g), row3(xa_k_norm_g), row3(a_h_norm_g)

    mem_k, mem_v = _memkv(mem, mem_norm_g.reshape(1, D_MODEL), bf(mem_w_kv))
    xf = x.reshape(n, D_MODEL)
    for i in range(DEPTH):
        j = i // 2
        if i % 2 == 0:
            xf, xq, y_tok = _mlstm_layer(xf, g_ffn1, ffn1_in, ffn1_out, g_mix, a_main, a_gw, a_gb,
                                         g_xq, g_xk, g_h, i, j, bsz, seq)
            xf = _outproj(xf, y_tok, xq, mem_k, mem_v, a_out, j, bsz, seq)
        else:
            xf, xq, y_tok = _swa_layer(xf, pos, b_sinks[j], g_ffn1, ffn1_in, ffn1_out, g_mix, b_main, b_gqk,
                                       rope_tab, g_xq, g_xk, i, j, bsz, seq)
            xf = _outproj(xf, y_tok, xq, mem_k, mem_v, b_out, j, bsz, seq)
        xf = _ffn(xf, g_ffn2, ffn2_in, ffn2_out, i)
    return xf.reshape(bsz, seq, D_MODEL)
```

```python
import functools
import math

import numpy as np
import jax
import jax.numpy as jnp
from jax import lax
from jax.experimental import pallas as pl
from jax.experimental.pallas import tpu as pltpu

F32 = jnp.float32
BF16 = jnp.bfloat16

D_MODEL = 1024
DEPTH = 4
EPS = 1e-6
D_FF = 2816

A_HEADS = 4
A_DQK = 128
A_DV = 256
A_QK = A_HEADS * A_DQK
A_V = A_HEADS * A_DV
A_GATES = 2 * A_HEADS
A_TOK = 2 * A_QK + 2 * A_V + A_GATES

B_HEADS = 16
B_KV_HEADS = 2
B_GROUP = B_HEADS // B_KV_HEADS
B_HD = 64
B_WINDOW = 128
B_Q = B_HEADS * B_HD
B_KV = B_KV_HEADS * B_HD
B_TOK = B_Q + 2 * B_KV
ROT_DIM = 16
ROT_HALF = ROT_DIM // 2
ROPE_THETA = 500000.0

N_MEM = 256
M_HEADS = 4
M_HD = 128
M_Q = M_HEADS * M_HD

LOG2E = math.log2(math.e)

LANES = 128
SUBLANES = 8
MXU_DIM = 256
VMEM_LIMIT = 56 * 1024 * 1024

ROW_TILE = 512
FF_TILE = MXU_DIM
FFN_ROWS = 1024
FFN_PART = 512
OUT_ROWS = 1024
MLSTM_CHUNK = 256

A_COL_Q, A_COL_K, A_COL_V, A_COL_O = 0, A_QK, 2 * A_QK, 2 * A_QK + A_V
A_COL_XQ = 2 * A_QK + 2 * A_V
A_MAIN = A_COL_XQ + M_Q
G_LANE_A, G_LANE_B, G_LANE_CM = 0, A_HEADS, 2 * A_HEADS
B_COL_XQ = B_Q
B_COL_KV = B_Q + M_Q
B_MAIN = B_COL_KV + 2 * B_KV


def _params(*semantics):
    return pltpu.CompilerParams(dimension_semantics=semantics, vmem_limit_bytes=VMEM_LIMIT)


def _resident(block_shape, index_map):
    return pl.BlockSpec(block_shape, index_map, pipeline_mode=pl.Buffered(1))


def _rms(x, gain):
    ms = jnp.mean(x * x, axis=-1, keepdims=True)
    return x * lax.rsqrt(ms + EPS) * gain


def _dot(a, b):
    return jnp.dot(a, b, preferred_element_type=F32)


def _dot_nt(a, b):
    return lax.dot_general(a, b, (((1,), (1,)), ((), ())), preferred_element_type=F32)


def _split2(x):
    hi = x.astype(BF16)
    return hi, (x - hi.astype(F32)).astype(BF16)


def _ffn_kernel(x_ref, g_ref, win_ref, wout_ref, o_ref):
    parts = [slice(r * FFN_PART, (r + 1) * FFN_PART) for r in range(FFN_ROWS // FFN_PART)]
    x = [x_ref[rows, :] for rows in parts]
    xn = [_rms(xr, g_ref[...]).astype(BF16) for xr in x]
    acc = [jnp.zeros_like(xr) for xr in x]
    for j in range(D_FF // FF_TILE):
        lo = j * FF_TILE
        for r in range(len(parts)):
            gate = _dot(xn[r], win_ref[:, lo:lo + FF_TILE])
            up = _dot(xn[r], win_ref[:, D_FF + lo:D_FF + lo + FF_TILE])
            h = (gate * jax.nn.sigmoid(gate) * up).astype(BF16)
            acc[r] = acc[r] + _dot(h, wout_ref[lo:lo + FF_TILE, :])
    for r, rows in enumerate(parts):
        o_ref[rows, :] = x[r] + 0.5 * acc[r]


def _ffn_stream(x, g_ref, win_ref, wout_ref):
    xn = _rms(x, g_ref[...]).astype(BF16)
    acc = [jnp.zeros_like(x)]

    def unit(j):
        lo = j * FF_TILE
        gate = _dot(xn, win_ref[:, lo:lo + FF_TILE])
        up = _dot(xn, win_ref[:, D_FF + lo:D_FF + lo + FF_TILE])
        h = (gate * jax.nn.sigmoid(gate) * up).astype(BF16)
        acc[0] = acc[0] + _dot(h, wout_ref[lo:lo + FF_TILE, :])

    return [functools.partial(unit, j) for j in range(D_FF // FF_TILE)], lambda: x + 0.5 * acc[0]


def _shares(n_items, n_takers):
    base, extra = divmod(n_items, n_takers)
    return [base + (i < extra) for i in range(n_takers)]


def _ffn(x, gains, w_in, w_out, layer):
    n = x.shape[0]
    return pl.pallas_call(
        _ffn_kernel,
        out_shape=jax.ShapeDtypeStruct((n, D_MODEL), F32),
        grid=(n // FFN_ROWS,),
        in_specs=[
            pl.BlockSpec((FFN_ROWS, D_MODEL), lambda i: (i, 0)),
            _resident((None, 1, D_MODEL), lambda i: (layer, 0, 0)),
            _resident((None, D_MODEL, 2 * D_FF), lambda i: (layer, 0, 0)),
            _resident((None, D_FF, D_MODEL), lambda i: (layer, 0, 0)),
        ],
        out_specs=pl.BlockSpec((FFN_ROWS, D_MODEL), lambda i: (i, 0)),
        compiler_params=_params("parallel"),
        name="ffn",
    )(x, gains, w_in, w_out)


def _memkv_kernel(mem_ref, g_ref, w_ref, k_ref, v_ref):
    mn = _rms(mem_ref[...], g_ref[...]).astype(BF16)
    kv = _dot(mn, w_ref[...])
    for h in range(M_HEADS):
        kh = kv[:, h * M_HD:(h + 1) * M_HD]
        ms = jnp.mean(kh * kh, axis=-1, keepdims=True)
        k_ref[h * M_HD:(h + 1) * M_HD, :] = (kh * lax.rsqrt(ms + EPS)).T.astype(BF16)
    v_ref[...] = kv[:, M_Q:].astype(BF16)


def _memkv(mem, gain, w_kv):
    bsz = mem.shape[0]
    return pl.pallas_call(
        _memkv_kernel,
        out_shape=(jax.ShapeDtypeStruct((bsz, M_Q, N_MEM), BF16),
                   jax.ShapeDtypeStruct((bsz, N_MEM, M_Q), BF16)),
        grid=(bsz,),
        in_specs=[
            pl.BlockSpec((None, N_MEM, D_MODEL), lambda b: (b, 0, 0)),
            _resident((1, D_MODEL), lambda b: (0, 0)),
            _resident((D_MODEL, 2 * M_Q), lambda b: (0, 0)),
        ],
        out_specs=(pl.BlockSpec((None, M_Q, N_MEM), lambda b: (b, 0, 0)),
                   pl.BlockSpec((None, N_MEM, M_Q), lambda b: (b, 0, 0))),
        compiler_params=_params("parallel"),
        name="memkv",
    )(mem, gain, w_kv)


def _store_mem_queries(xq, gq_ref, gk_ref, out_ref, col0):
    geff = gq_ref[...] * gk_ref[...] * (M_HD ** -0.5)
    for h in range(xq.shape[1] // M_HD):
        s = xq[:, h * M_HD:(h + 1) * M_HD]
        ms = jnp.mean(s * s, axis=-1, keepdims=True)
        out_ref[:, col0 + h * M_HD:col0 + (h + 1) * M_HD] = (s * lax.rsqrt(ms + EPS) * geff).astype(BF16)


def _log_sigmoid(x):
    return jnp.minimum(x, 0.0) - jnp.log(1.0 + jnp.exp(-jnp.abs(x)))


def _causal_mask(n):
    row = lax.broadcasted_iota(jnp.int32, (n, n), 0)
    col = lax.broadcasted_iota(jnp.int32, (n, n), 1)
    return col <= row


def _mlstm_gate_record(g):
    chunk = g.shape[0]
    causal = _causal_mask(chunk)
    hi, lo = _split2(_log_sigmoid(g))
    both = _dot(causal.astype(BF16), jnp.concatenate([hi, lo], axis=1))
    b = both[:, :LANES] + both[:, LANES:]
    a = g - pltpu.roll(b, LANES - A_HEADS, axis=1)
    a_t = a.T
    lane = lax.broadcasted_iota(jnp.int32, (chunk, LANES), 1)
    rec = jnp.where(lane < G_LANE_B, a, jnp.where(lane < G_LANE_CM, b, 0.0))
    for h in range(A_HEADS):
        run_max = jnp.max(jnp.where(causal, a_t[h:h + 1, :], -jnp.inf), axis=1, keepdims=True)
        rec = jnp.where(lane == G_LANE_CM + h, run_max, rec)
    return rec


ROPE_PAD = B_HD - ROT_DIM


def _rope_projector(pos_ref, gqk_ref, tab_ref, out_ref):
    ang = tab_ref[:, 0:1] * pos_ref[...].astype(F32)
    cos_c = jnp.cos(ang)
    sin_c = jnp.sin(ang) * tab_ref[:, LANES:LANES + 1]
    pad = jnp.zeros((ROPE_PAD, ang.shape[1]), F32)
    cos_t = jnp.concatenate([cos_c, pad, cos_c, pad], axis=0).T
    sin_t = jnp.concatenate([sin_c, pad, sin_c, pad], axis=0).T
    lane = lax.broadcasted_iota(jnp.int32, (1, LANES), 1) % B_HD
    cos_t = cos_t + (lane >= ROT_DIM).astype(F32)
    first_half = lane < ROT_HALF
    r = lax.broadcasted_iota(jnp.int32, (MXU_DIM, MXU_DIM), 0) // B_HD
    c = lax.broadcasted_iota(jnp.int32, (MXU_DIM, MXU_DIM), 1) // B_HD
    seg = jnp.where(r == c, 1.0 / B_HD, 0.0).astype(BF16)

    def norm_rope(slab, cos_g, sin_g, col0):
        hi, lo = _split2(slab * slab)
        inv = lax.rsqrt(_dot(hi, seg) + _dot(lo, seg) + EPS)
        for half in range(MXU_DIM // LANES):
            sl = slice(half * LANES, (half + 1) * LANES)
            xh = slab[:, sl]
            partner = jnp.where(first_half, pltpu.roll(xh, LANES - ROT_HALF, axis=1),
                                pltpu.roll(xh, ROT_HALF, axis=1))
            out = (xh * cos_g + partner * sin_g) * inv[:, sl]
            out_ref[:, col0 + half * LANES:col0 + (half + 1) * LANES] = out.astype(BF16)

    q_scale = B_HD ** -0.5 * LOG2E
    cos_q, sin_q = cos_t * (gqk_ref[0:1, :] * q_scale), sin_t * (gqk_ref[1:2, :] * q_scale)
    cos_k, sin_k = cos_t * gqk_ref[2:3, :], sin_t * gqk_ref[3:4, :]
    return norm_rope, (cos_q, sin_q, cos_k, sin_k)


A_STATE_W = A_DV + LANES


def _mlstm_chunk_setup(rec, k, m_sc):
    chunk = rec.shape[0]
    scale = A_DQK ** -0.5
    b = pltpu.roll(rec, LANES - G_LANE_B, axis=1)
    run_max = pltpu.roll(rec, LANES - G_LANE_CM, axis=1)
    m_row = m_sc[0:1, :]
    big_m = jnp.maximum(run_max, m_row)
    m_last = big_m[chunk - 1:chunk, :]
    m_sc[0:1, :] = b[chunk - 1:chunk, :] + m_last
    return dict(
        a_t=rec.T,
        big_m=big_m,
        inter=scale * jnp.exp(m_row - big_m),
        floor=jnp.exp(-(b + big_m)),
        m_last=m_last,
        decay=jnp.exp(m_row - m_last),
        k_t=k.astype(F32).T,
        causal=_causal_mask(chunk),
        ones_col=(lax.broadcasted_iota(jnp.int32, (chunk, LANES), 1) == 0).astype(BF16),
    )


def _mlstm_head(h, c, qh, vh, og, gain, ct_sc, between):
    scale = A_DQK ** -0.5
    a_row = c["a_t"][h:h + 1, :]
    k_t = c["k_t"][h * A_DQK:(h + 1) * A_DQK, :]
    qk = _dot(qh, k_t.astype(BF16))
    v_ext = jnp.concatenate([vh, c["ones_col"]], axis=1)
    state = ct_sc[h]
    w_row = jnp.exp(a_row - c["m_last"][:, h:h + 1])
    kw_t = (k_t * w_row).astype(BF16)
    ct_sc[h] = c["decay"][:, h:h + 1] * state + _dot(kw_t, v_ext)
    between()
    big_m = c["big_m"][:, h:h + 1]
    decay_mat = jnp.exp(jnp.where(c["causal"], (a_row + math.log(scale)) - big_m, -jnp.inf))
    p = (qk * decay_mat).astype(BF16)
    q_inter = (qh.astype(F32) * c["inter"][:, h:h + 1]).astype(BF16)
    tot = _dot(jnp.concatenate([p, q_inter], axis=1),
               jnp.concatenate([v_ext, state.astype(BF16)], axis=0))
    num = tot[:, :A_DV]
    den = tot[:, A_DV:A_DV + 1]
    inv = 1.0 / jnp.maximum(jnp.abs(den), c["floor"][:, h:h + 1])
    ms = jnp.mean(num * num, axis=1, keepdims=True)
    fac = inv * lax.rsqrt(inv * inv * ms + EPS)
    return (num * fac * gain * og.astype(F32)).astype(BF16)


A_PROJ_W = A_COL_XQ
A_UNIT_W = MXU_DIM


def _mlstm_layer_kernel(blocks_per_row, x_ref, g1_ref, win_ref, wout_ref, g_ref, w_ref, wg_ref, gb_ref,
                        gq_ref, gk_ref, hg_ref, x1_ref, xq_ref, y_ref, proj_sc, rec_sc, ct_sc, m_sc):
    s = pl.program_id(0)

    @pl.when(s == 0)
    def _():
        proj_sc[...] = jnp.zeros_like(proj_sc)
        rec_sc[...] = jnp.zeros_like(rec_sc)

    @pl.when(lax.rem(jnp.maximum(s - 1, 0), blocks_per_row) == 0)
    def _():
        ct_sc[...] = jnp.zeros_like(ct_sc)
        m_sc[...] = jnp.zeros_like(m_sc)

    put = proj_sc.at[lax.rem(s, 2)]
    put_rec = rec_sc.at[lax.rem(s, 2)]
    get = proj_sc.at[lax.rem(s + 1, 2)]
    get_rec = rec_sc.at[lax.rem(s + 1, 2)]

    ffn_units, ffn_result = _ffn_stream(x_ref[...], g1_ref, win_ref, wout_ref)
    xn = []

    def project(c):
        cols = slice(c * A_UNIT_W, (c + 1) * A_UNIT_W)
        y = _dot(xn[0], w_ref[:, cols])
        if cols.start >= A_COL_XQ:
            _store_mem_queries(y, gq_ref, gk_ref, xq_ref, cols.start - A_COL_XQ)
            return
        if cols.start >= A_COL_O:
            y = jax.nn.sigmoid(y)
        put[:, cols] = y.astype(BF16)

    chunks = [slice(c * MLSTM_CHUNK, (c + 1) * MLSTM_CHUNK) for c in range(ROW_TILE // MLSTM_CHUNK)]
    setups = {}

    def recur(c, h, between):
        rows = chunks[c]
        if h == 0:
            setups[c] = _mlstm_chunk_setup(get_rec[rows, :], get[rows, A_COL_K:A_COL_K + A_QK], m_sc)
        qk = slice(h * A_DQK, (h + 1) * A_DQK)
        v = slice(h * A_DV, (h + 1) * A_DV)
        y_ref[rows, v] = _mlstm_head(
            h, setups[c], get[rows, A_COL_Q + qk.start:A_COL_Q + qk.stop],
            get[rows, A_COL_V + v.start:A_COL_V + v.stop],
            get[rows, A_COL_O + v.start:A_COL_O + v.stop], hg_ref[:, v], ct_sc, between)

    def start_projection():
        x1 = ffn_result()
        x1_ref[...] = x1
        xn.append(_rms(x1, g_ref[...]).astype(BF16))
        gates = _dot(xn[0], wg_ref[...]) + gb_ref[...]
        project(0)
        project(1)
        for rows in chunks:
            put_rec[rows, :] = _mlstm_gate_record(gates[rows])

    stream = ffn_units + [start_projection] + [functools.partial(project, u) for u in range(2, A_MAIN // A_UNIT_W)]
    heads = [(c, h) for c in range(len(chunks)) for h in range(A_HEADS)]
    for (c, h), size in zip(heads, _shares(len(stream), len(heads))):
        share, stream = stream[:size], stream[size:]
        recur(c, h, lambda share=share: [unit() for unit in share])


def _mlstm_layer(x, ffn_g, ffn_in, ffn_out, mix_g, w_main, w_gates, gate_b, gq, gk, h_gain, layer, j, bsz, seq):
    n = bsz * seq
    nb = n // ROW_TILE
    last = nb - 1
    return pl.pallas_call(
        functools.partial(_mlstm_layer_kernel, seq // ROW_TILE),
        out_shape=(jax.ShapeDtypeStruct((n, D_MODEL), F32),
                   jax.ShapeDtypeStruct((n, M_Q), BF16),
                   jax.ShapeDtypeStruct((n, A_V), BF16)),
        grid=(nb + 1,),
        in_specs=[
            pl.BlockSpec((ROW_TILE, D_MODEL), lambda s: (jnp.minimum(s, last), 0)),
            _resident((None, 1, D_MODEL), lambda s: (layer, 0, 0)),
            _resident((None, D_MODEL, 2 * D_FF), lambda s: (layer, 0, 0)),
            _resident((None, D_FF, D_MODEL), lambda s: (layer, 0, 0)),
            _resident((None, 1, D_MODEL), lambda s: (layer, 0, 0)),
            _resident((None, D_MODEL, A_MAIN), lambda s: (j, 0, 0)),
            _resident((None, D_MODEL, LANES), lambda s: (j, 0, 0)),
            _resident((None, 1, LANES), lambda s: (j, 0, 0)),
            _resident((None, 1, M_HD), lambda s: (layer, 0, 0)),
            _resident((None, 1, M_HD), lambda s: (layer, 0, 0)),
            _resident((None, 1, A_V), lambda s: (j, 0, 0)),
        ],
        out_specs=(pl.BlockSpec((ROW_TILE, D_MODEL), lambda s: (jnp.minimum(s, last), 0)),
                   pl.BlockSpec((ROW_TILE, M_Q), lambda s: (jnp.minimum(s, last), 0)),
                   pl.BlockSpec((ROW_TILE, A_V), lambda s: (jnp.maximum(s - 1, 0), 0))),
        scratch_shapes=[pltpu.VMEM((2, ROW_TILE, A_PROJ_W), BF16),
                        pltpu.VMEM((2, ROW_TILE, LANES), F32),
                        pltpu.VMEM((A_HEADS, A_DQK, A_STATE_W), F32),
                        pltpu.VMEM((SUBLANES, LANES), F32)],
        compiler_params=_params("arbitrary"),
        name="mlstm_layer",
    )(x, ffn_g, ffn_in, ffn_out, mix_g, w_main, w_gates, gate_b, gq, gk, h_gain)


SWA_SUB = B_HD
SWA_SPAN = B_WINDOW + SWA_SUB


def _swa_units(sink_ref, q_ref, kv, first_block, y_ref):
    sub_q, span_k = SWA_SUB, SWA_SPAN
    pairs = B_GROUP // 2
    keys = kv[:, :B_KV]
    vals = kv[:, B_KV:]
    lane_kv = lax.broadcasted_iota(jnp.int32, keys.shape, 1)
    lane_q = lax.broadcasted_iota(jnp.int32, (sub_q, LANES), 1)
    kk = lax.broadcasted_iota(jnp.int32, (span_k, LANES), 0)
    qq = lax.broadcasted_iota(jnp.int32, (span_k, LANES), 1) % sub_q
    dist = B_WINDOW + qq - kk
    in_window = (dist >= 0) & (dist < B_WINDOW)
    odd_lane = lax.broadcasted_iota(jnp.int32, (1, LANES), 1) >= sub_q

    def own_half(x, grp):
        return jnp.where((lane_kv // B_HD) == grp, x, 0.0)

    def in_half(x, grp, half):
        own = own_half(x, grp)
        return own if grp == half else pltpu.roll(own, B_HD, axis=1)

    kg = [[in_half(keys, grp, half).astype(BF16) for half in range(2)] for grp in range(B_KV_HEADS)]
    vg_t = [(own_half(vals, grp) + pltpu.roll(own_half(vals, grp), B_HD, axis=1)).T.astype(BF16)
            for grp in range(B_KV_HEADS)]

    def tile_heads(grp, i):
        parity, first_pair = divmod(i, pairs // 2)
        first_pair *= 2
        return [grp * B_GROUP + 2 * (first_pair + k) + parity for k in range(2)]

    n_tiles = B_GROUP * sub_q // LANES
    sink_rows = [[jnp.where(odd_lane, sink_ref[hb], sink_ref[ha]) * LOG2E
                  for ha, hb in [tile_heads(grp, i) for i in range(n_tiles)]] for grp in range(B_KV_HEADS)]

    def scores_of(grp, t):
        stack = jnp.concatenate(
            [q_ref[t * sub_q:(t + 1) * sub_q, (grp * pairs + pp) * LANES:(grp * pairs + pp + 1) * LANES]
             for pp in range(pairs)], axis=0)
        span = slice(t * sub_q, t * sub_q + span_k)
        return [_dot_nt(kg[grp][half][span], stack) for half in range(2)]

    units = [(grp, t) for grp in range(B_KV_HEADS) for t in range(ROW_TILE // sub_q)]
    ahead = [scores_of(*units[0])]

    def unit(u, between):
        grp, t = units[u]
        scores_t = ahead[0]
        if u + 1 < len(units):
            ahead[0] = scores_of(*units[u + 1])
        between()
        rows = slice(t * sub_q, (t + 1) * sub_q)
        span = slice(t * sub_q, t * sub_q + span_k)
        valid = in_window
        if t * sub_q < B_WINDOW:
            valid = valid & ((kk >= B_WINDOW - t * sub_q) | jnp.logical_not(first_block))
        tiles = [sc[:, k * LANES:(k + 1) * LANES] for sc in scores_t for k in range(n_tiles // 2)]
        probs = []
        inv = []
        for i, tile in enumerate(tiles):
            sink = sink_rows[grp][i]
            s = jnp.where(valid, tile, -jnp.inf)
            m = jnp.maximum(jnp.max(s, axis=0, keepdims=True), sink)
            p = jnp.exp2(s - m)
            inv.append(1.0 / (jnp.sum(p, axis=0, keepdims=True) + jnp.exp2(sink - m)))
            probs.append(p.astype(BF16))
        out_t = _dot(vg_t[grp][:, span], jnp.concatenate(probs, axis=1))
        o = [(out_t[:, i * LANES:(i + 1) * LANES] * inv[i]).T for i in range(n_tiles)]
        for pp in range(pairs):
            even, odd = o[pp // 2], o[n_tiles // 2 + pp // 2]
            r0 = (pp % 2) * sub_q
            c0 = (grp * pairs + pp) * LANES
            y_ref[rows, c0:c0 + LANES] = jnp.where(lane_q < B_HD, even[r0:r0 + sub_q],
                                                   odd[r0:r0 + sub_q]).astype(BF16)

    return [functools.partial(unit, u) for u in range(len(units))]


B_PROJ_W = B_Q + 2 * B_KV


def _swa_layer_kernel(blocks_per_row, sink_ref, x_ref, pos_ref, g1_ref, win_ref, wout_ref, g_ref, w_ref,
                      gqk_ref, tab_ref, gq_ref, gk_ref, x1_ref, xq_ref, y_ref, proj_sc):
    s = pl.program_id(0)

    @pl.when(s == 0)
    def _():
        proj_sc[...] = jnp.zeros_like(proj_sc)

    put = proj_sc.at[lax.rem(s, 3)]
    get = proj_sc.at[lax.rem(s + 2, 3)]
    before = proj_sc.at[lax.rem(s + 1, 3)]
    first_block = lax.rem(jnp.maximum(s - 1, 0), blocks_per_row) == 0

    kv = jnp.concatenate([before[ROW_TILE - B_WINDOW:, B_Q:B_PROJ_W], get[:, B_Q:B_PROJ_W]], axis=0).astype(F32)
    attend = _swa_units(sink_ref, get, kv, first_block, y_ref)

    ffn_units, ffn_result = _ffn_stream(x_ref[...], g1_ref, win_ref, wout_ref)
    norm_rope, tables = _rope_projector(pos_ref, gqk_ref, tab_ref, put)
    cos_q, sin_q, cos_k, sin_k = tables
    n_q = B_Q // MXU_DIM
    cols = [s_ * MXU_DIM for s_ in range(n_q)] + [B_COL_KV, B_COL_XQ, B_COL_XQ + MXU_DIM]
    xn = []
    ahead = []

    def start_projection():
        x1 = ffn_result()
        x1_ref[...] = x1
        xn.append(_rms(x1, g_ref[...]).astype(BF16))
        ahead.append(_dot(xn[0], w_ref[:, cols[0]:cols[0] + MXU_DIM]))

    def project(i):
        slab = ahead[0]
        if i + 1 < len(cols):
            ahead[0] = _dot(xn[0], w_ref[:, cols[i + 1]:cols[i + 1] + MXU_DIM])
        if i < n_q:
            norm_rope(slab, cos_q, sin_q, i * MXU_DIM)
        elif i == n_q:
            v = slab[:, B_KV:]
            norm_rope(slab, cos_k, sin_k, B_Q)
            put[:, B_Q + B_KV:B_PROJ_W] = v.astype(BF16)
        else:
            _store_mem_queries(slab, gq_ref, gk_ref, xq_ref, cols[i] - B_COL_XQ)

    stream = ffn_units + [start_projection] + [functools.partial(project, i) for i in range(len(cols))]
    for unit, size in zip(attend, _shares(len(stream), len(attend))):
        share, stream = stream[:size], stream[size:]
        unit(lambda share=share: [item() for item in share])


def _swa_layer(x, pos, sinks, ffn_g, ffn_in, ffn_out, mix_g, w_main, gqk, rope_tab, gq, gk, layer, j, bsz, seq):
    n = bsz * seq
    nb = n // ROW_TILE
    last = nb - 1
    return pl.pallas_call(
        functools.partial(_swa_layer_kernel, seq // ROW_TILE),
        out_shape=(jax.ShapeDtypeStruct((n, D_MODEL), F32),
                   jax.ShapeDtypeStruct((n, M_Q), BF16),
                   jax.ShapeDtypeStruct((n, B_Q), BF16)),
        grid=(nb + 1,),
        in_specs=[
            pl.BlockSpec(memory_space=pltpu.SMEM),
            pl.BlockSpec((ROW_TILE, D_MODEL), lambda s: (jnp.minimum(s, last), 0)),
            pl.BlockSpec((None, 1, ROW_TILE), lambda s: (jnp.minimum(s, last), 0, 0)),
            _resident((None, 1, D_MODEL), lambda s: (layer, 0, 0)),
            _resident((None, D_MODEL, 2 * D_FF), lambda s: (layer, 0, 0)),
            _resident((None, D_FF, D_MODEL), lambda s: (layer, 0, 0)),
            _resident((None, 1, D_MODEL), lambda s: (layer, 0, 0)),
            _resident((None, D_MODEL, B_MAIN), lambda s: (j, 0, 0)),
            _resident((None, 4, LANES), lambda s: (j, 0, 0)),
            _resident((ROT_DIM, 2 * LANES), lambda s: (0, 0)),
            _resident((None, 1, M_HD), lambda s: (layer, 0, 0)),
            _resident((None, 1, M_HD), lambda s: (layer, 0, 0)),
        ],
        out_specs=(pl.BlockSpec((ROW_TILE, D_MODEL), lambda s: (jnp.minimum(s, last), 0)),
                   pl.BlockSpec((ROW_TILE, M_Q), lambda s: (jnp.minimum(s, last), 0)),
                   pl.BlockSpec((ROW_TILE, B_Q), lambda s: (jnp.maximum(s - 1, 0), 0))),
        scratch_shapes=[pltpu.VMEM((3, ROW_TILE, B_PROJ_W), BF16)],
        compiler_params=_params("arbitrary"),
        name="swa_layer",
    )(sinks, x, pos, ffn_g, ffn_in, ffn_out, mix_g, w_main, gqk, rope_tab, gq, gk)


def _outproj_kernel(x_ref, yt_ref, xq_ref, mkt_ref, mv_ref, w_ref, o_ref):
    y_dim = yt_ref.shape[1]
    head_cols = [slice(h * M_HD, (h + 1) * M_HD) for h in range(M_HEADS)]
    for r in range(OUT_ROWS // ROW_TILE):
        rows = slice(r * ROW_TILE, (r + 1) * ROW_TILE)
        scores = [_dot(xq_ref[rows, sl], mkt_ref[sl, :]) for sl in head_cols]
        acc = x_ref[rows, :] + _dot(yt_ref[rows, :], w_ref[:y_dim, :])
        heads = []
        for s, sl in zip(scores, head_cols):
            p = jnp.exp(s - jnp.max(s, axis=1, keepdims=True))
            inv = 1.0 / jnp.sum(p, axis=1, keepdims=True)
            heads.append((_dot(p.astype(BF16), mv_ref[:, sl]) * inv).astype(BF16))
        o_ref[rows, :] = acc + _dot(jnp.concatenate(heads, axis=1), w_ref[y_dim:, :])


def _outproj(x, y_tok, xq, mem_k, mem_v, w_out, j, bsz, seq):
    nt = seq // OUT_ROWS
    y_dim = y_tok.shape[1]
    return pl.pallas_call(
        _outproj_kernel,
        out_shape=jax.ShapeDtypeStruct((bsz * seq, D_MODEL), F32),
        grid=(bsz, nt),
        in_specs=[
            pl.BlockSpec((OUT_ROWS, D_MODEL), lambda b, t: (b * nt + t, 0)),
            pl.BlockSpec((OUT_ROWS, y_dim), lambda b, t: (b * nt + t, 0)),
            pl.BlockSpec((OUT_ROWS, M_Q), lambda b, t: (b * nt + t, 0)),
            pl.BlockSpec((None, M_Q, N_MEM), lambda b, t: (b, 0, 0)),
            pl.BlockSpec((None, N_MEM, M_Q), lambda b, t: (b, 0, 0)),
            _resident((None, y_dim + M_Q, D_MODEL), lambda b, t: (j, 0, 0)),
        ],
        out_specs=pl.BlockSpec((OUT_ROWS, D_MODEL), lambda b, t: (b * nt + t, 0)),
        compiler_params=_params("parallel", "parallel"),
        name="outproj",
    )(x, y_tok, xq, mem_k, mem_v, w_out)


def _rope_row_tables():
    inv_freq = ROPE_THETA ** (-np.arange(0, ROT_DIM, 2, dtype=np.float32) / ROT_DIM)
    tab = np.zeros((ROT_DIM, 2 * LANES), np.float32)
    tab[:, :LANES] = np.concatenate([inv_freq, inv_freq])[:, None]
    tab[:, LANES:] = np.where(np.arange(ROT_DIM) < ROT_HALF, -1.0, 1.0)[:, None]
    return jnp.asarray(tab)


def _rope_gain_rows(gain):
    d = np.arange(B_HD)
    partner = np.where(d < ROT_HALF, d + ROT_HALF, np.where(d < ROT_DIM, d - ROT_HALF, d))
    reps = LANES // B_HD
    return jnp.stack([jnp.tile(gain, (1, reps)), jnp.tile(gain[:, partner], (1, reps))], axis=1)


def kernel(x, mem, positions, mem_norm_g, mem_w_kv, ffn1_norm_g, ffn1_w_in, ffn1_w_out,
           mix_norm_g, ffn2_norm_g, ffn2_w_in, ffn2_w_out, xa_q_norm_g, xa_k_norm_g,
           a_w_in, a_gate_b, a_h_norm_g, a_w_out,
           b_w_in, b_q_norm_g, b_k_norm_g, b_sinks, b_w_out):
    bsz, seq, _ = x.shape
    n = bsz * seq
    assert seq % OUT_ROWS == 0 and OUT_ROWS % ROW_TILE == 0 and n % FFN_ROWS == 0
    assert ROW_TILE % MLSTM_CHUNK == 0 and ROW_TILE % SWA_SUB == 0

    bf = lambda w: w.astype(BF16)
    row3 = lambda g: g.reshape(g.shape[0], 1, g.shape[1])
    ffn1_in, ffn1_out, ffn2_in, ffn2_out = bf(ffn1_w_in), bf(ffn1_w_out), bf(ffn2_w_in), bf(ffn2_w_out)
    a_main = bf(jnp.concatenate([a_w_in[..., :A_COL_XQ], a_w_in[..., A_TOK:]], axis=-1))
    a_gw = bf(jnp.pad(a_w_in[..., A_COL_XQ:A_TOK], ((0, 0), (0, 0), (0, LANES - A_GATES))))
    a_gb = row3(jnp.pad(a_gate_b, ((0, 0), (0, LANES - A_GATES))))
    b_main = bf(jnp.concatenate([b_w_in[..., :B_Q], b_w_in[..., B_TOK:], b_w_in[..., B_Q:B_TOK]], axis=-1))
    b_gqk = jnp.concatenate([_rope_gain_rows(b_q_norm_g), _rope_gain_rows(b_k_norm_g)], axis=1)
    a_out, b_out = bf(a_w_out), bf(b_w_out)
    rope_tab = _rope_row_tables()
    pos = positions.reshape(n // ROW_TILE, 1, ROW_TILE)
    g_ffn1, g_mix, g_ffn2 = row3(ffn1_norm_g), row3(mix_norm_g), row3(ffn2_norm_g)
    g_xq, g_xk, g_h = row3(xa_q_norm_g), row3(xa_k_norm_g), row3(a_h_norm_g)

    mem_k, mem_v = _memkv(mem, mem_norm_g.reshape(1, D_MODEL), bf(mem_w_kv))
    xf = x.reshape(n, D_MODEL)
    for i in range(DEPTH):
        j = i // 2
        if i % 2 == 0:
            xf, xq, y_tok = _mlstm_layer(xf, g_ffn1, ffn1_in, ffn1_out, g_mix, a_main, a_gw, a_gb,
                                         g_xq, g_xk, g_h, i, j, bsz, seq)
            xf = _outproj(xf, y_tok, xq, mem_k, mem_v, a_out, j, bsz, seq)
        else:
            xf, xq, y_tok = _swa_layer(xf, pos, b_sinks[j], g_ffn1, ffn1_in, ffn1_out, g_mix, b_main, b_gqk,
                                       rope_tab, g_xq, g_xk, i, j, bsz, seq)
            xf = _outproj(xf, y_tok, xq, mem_k, mem_v, b_out, j, bsz, seq)
        xf = _ffn(xf, g_ffn2, ffn2_in, ffn2_out, i)
    return xf.reshape(bsz, seq, D_MODEL)
```

```python
import functools
import math

import numpy as np
import jax
import jax.numpy as jnp
from jax import lax
from jax.experimental import pallas as pl
from jax.experimental.pallas import tpu as pltpu

F32 = jnp.float32
BF16 = jnp.bfloat16

D_MODEL = 1024
DEPTH = 4
EPS = 1e-6
D_FF = 2816

A_HEADS = 4
A_DQK = 128
A_DV = 256
A_QK = A_HEADS * A_DQK
A_V = A_HEADS * A_DV
A_GATES = 2 * A_HEADS
A_TOK = 2 * A_QK + 2 * A_V + A_GATES

B_HEADS = 16
B_KV_HEADS = 2
B_GROUP = B_HEADS // B_KV_HEADS
B_HD = 64
B_WINDOW = 128
B_Q = B_HEADS * B_HD
B_KV = B_KV_HEADS * B_HD
B_TOK = B_Q + 2 * B_KV
ROT_DIM = 16
ROT_HALF = ROT_DIM // 2
ROPE_THETA = 500000.0

N_MEM = 256
M_HEADS = 4
M_HD = 128
M_Q = M_HEADS * M_HD

LOG2E = math.log2(math.e)

LANES = 128
SUBLANES = 8
MXU_DIM = 256
VMEM_LIMIT = 56 * 1024 * 1024

ROW_TILE = 512
FF_TILE = MXU_DIM
OUT_ROWS = 1024
MLSTM_CHUNK = 256

A_COL_Q, A_COL_K, A_COL_V, A_COL_O = 0, A_QK, 2 * A_QK, 2 * A_QK + A_V
A_COL_XQ = 2 * A_QK + 2 * A_V
A_MAIN = A_COL_XQ + M_Q
G_LANE_A, G_LANE_B, G_LANE_CM = 0, A_HEADS, 2 * A_HEADS
B_COL_XQ = B_Q
B_COL_KV = B_Q + M_Q
B_MAIN = B_COL_KV + 2 * B_KV


def _params(*semantics):
    return pltpu.CompilerParams(dimension_semantics=semantics, vmem_limit_bytes=VMEM_LIMIT)


def _resident(block_shape, index_map):
    return pl.BlockSpec(block_shape, index_map, pipeline_mode=pl.Buffered(1))


def _rms(x, gain):
    ms = jnp.mean(x * x, axis=-1, keepdims=True)
    return x * lax.rsqrt(ms + EPS) * gain


def _dot(a, b):
    return jnp.dot(a, b, preferred_element_type=F32)


def _dot_nt(a, b):
    return lax.dot_general(a, b, (((1,), (1,)), ((), ())), preferred_element_type=F32)


def _split2(x):
    hi = x.astype(BF16)
    return hi, (x - hi.astype(F32)).astype(BF16)


def _ffn_stream(x, g_ref, win_ref, wout_ref):
    xn = _rms(x, g_ref[...]).astype(BF16)
    acc = [jnp.zeros_like(x)]

    def unit(j):
        lo = j * FF_TILE
        gate = _dot(xn, win_ref[:, lo:lo + FF_TILE])
        up = _dot(xn, win_ref[:, D_FF + lo:D_FF + lo + FF_TILE])
        h = (gate * jax.nn.sigmoid(gate) * up).astype(BF16)
        acc[0] = acc[0] + _dot(h, wout_ref[lo:lo + FF_TILE, :])

    return [functools.partial(unit, j) for j in range(D_FF // FF_TILE)], lambda: x + 0.5 * acc[0]


def _shares(n_items, n_takers):
    base, extra = divmod(n_items, n_takers)
    return [base + (i < extra) for i in range(n_takers)]


def _memkv_kernel(mem_ref, g_ref, w_ref, k_ref, v_ref):
    mn = _rms(mem_ref[...], g_ref[...]).astype(BF16)
    kv = _dot(mn, w_ref[...])
    for h in range(M_HEADS):
        kh = kv[:, h * M_HD:(h + 1) * M_HD]
        ms = jnp.mean(kh * kh, axis=-1, keepdims=True)
        k_ref[h * M_HD:(h + 1) * M_HD, :] = (kh * lax.rsqrt(ms + EPS)).T.astype(BF16)
    v_ref[...] = kv[:, M_Q:].astype(BF16)


def _memkv(mem, gain, w_kv):
    bsz = mem.shape[0]
    return pl.pallas_call(
        _memkv_kernel,
        out_shape=(jax.ShapeDtypeStruct((bsz, M_Q, N_MEM), BF16),
                   jax.ShapeDtypeStruct((bsz, N_MEM, M_Q), BF16)),
        grid=(bsz,),
        in_specs=[
            pl.BlockSpec((None, N_MEM, D_MODEL), lambda b: (b, 0, 0)),
            _resident((1, D_MODEL), lambda b: (0, 0)),
            _resident((D_MODEL, 2 * M_Q), lambda b: (0, 0)),
        ],
        out_specs=(pl.BlockSpec((None, M_Q, N_MEM), lambda b: (b, 0, 0)),
                   pl.BlockSpec((None, N_MEM, M_Q), lambda b: (b, 0, 0))),
        compiler_params=_params("parallel"),
        name="memkv",
    )(mem, gain, w_kv)


def _store_mem_queries(xq, gq_ref, gk_ref, out_ref, col0):
    geff = gq_ref[...] * gk_ref[...] * (M_HD ** -0.5)
    for h in range(xq.shape[1] // M_HD):
        s = xq[:, h * M_HD:(h + 1) * M_HD]
        ms = jnp.mean(s * s, axis=-1, keepdims=True)
        out_ref[:, col0 + h * M_HD:col0 + (h + 1) * M_HD] = (s * lax.rsqrt(ms + EPS) * geff).astype(BF16)


def _log_sigmoid(x):
    return jnp.minimum(x, 0.0) - jnp.log(1.0 + jnp.exp(-jnp.abs(x)))


def _causal_mask(n):
    row = lax.broadcasted_iota(jnp.int32, (n, n), 0)
    col = lax.broadcasted_iota(jnp.int32, (n, n), 1)
    return col <= row


def _mlstm_gate_record(g):
    chunk = g.shape[0]
    causal = _causal_mask(chunk)
    hi, lo = _split2(_log_sigmoid(g))
    both = _dot(causal.astype(BF16), jnp.concatenate([hi, lo], axis=1))
    b = both[:, :LANES] + both[:, LANES:]
    a = g - pltpu.roll(b, LANES - A_HEADS, axis=1)
    a_t = a.T
    lane = lax.broadcasted_iota(jnp.int32, (chunk, LANES), 1)
    rec = jnp.where(lane < G_LANE_B, a, jnp.where(lane < G_LANE_CM, b, 0.0))
    for h in range(A_HEADS):
        run_max = jnp.max(jnp.where(causal, a_t[h:h + 1, :], -jnp.inf), axis=1, keepdims=True)
        rec = jnp.where(lane == G_LANE_CM + h, run_max, rec)
    return rec


ROPE_PAD = B_HD - ROT_DIM


def _rope_projector(pos_ref, gqk_ref, tab_ref, out_ref):
    ang = tab_ref[:, 0:1] * pos_ref[...].astype(F32)
    cos_c = jnp.cos(ang)
    sin_c = jnp.sin(ang) * tab_ref[:, LANES:LANES + 1]
    pad = jnp.zeros((ROPE_PAD, ang.shape[1]), F32)
    cos_t = jnp.concatenate([cos_c, pad, cos_c, pad], axis=0).T
    sin_t = jnp.concatenate([sin_c, pad, sin_c, pad], axis=0).T
    lane = lax.broadcasted_iota(jnp.int32, (1, LANES), 1) % B_HD
    cos_t = cos_t + (lane >= ROT_DIM).astype(F32)
    first_half = lane < ROT_HALF
    r = lax.broadcasted_iota(jnp.int32, (MXU_DIM, MXU_DIM), 0) // B_HD
    c = lax.broadcasted_iota(jnp.int32, (MXU_DIM, MXU_DIM), 1) // B_HD
    seg = jnp.where(r == c, 1.0 / B_HD, 0.0).astype(BF16)

    def norm_rope(slab, cos_g, sin_g, col0):
        hi, lo = _split2(slab * slab)
        inv = lax.rsqrt(_dot(hi, seg) + _dot(lo, seg) + EPS)
        for half in range(MXU_DIM // LANES):
            sl = slice(half * LANES, (half + 1) * LANES)
            xh = slab[:, sl]
            partner = jnp.where(first_half, pltpu.roll(xh, LANES - ROT_HALF, axis=1),
                                pltpu.roll(xh, ROT_HALF, axis=1))
            out = (xh * cos_g + partner * sin_g) * inv[:, sl]
            out_ref[:, col0 + half * LANES:col0 + (half + 1) * LANES] = out.astype(BF16)

    q_scale = B_HD ** -0.5 * LOG2E
    cos_q, sin_q = cos_t * (gqk_ref[0:1, :] * q_scale), sin_t * (gqk_ref[1:2, :] * q_scale)
    cos_k, sin_k = cos_t * gqk_ref[2:3, :], sin_t * gqk_ref[3:4, :]
    return norm_rope, (cos_q, sin_q, cos_k, sin_k)


A_STATE_W = A_DV + LANES


def _mlstm_chunk_setup(rec, k, m_sc):
    chunk = rec.shape[0]
    scale = A_DQK ** -0.5
    b = pltpu.roll(rec, LANES - G_LANE_B, axis=1)
    run_max = pltpu.roll(rec, LANES - G_LANE_CM, axis=1)
    m_row = m_sc[0:1, :]
    big_m = jnp.maximum(run_max, m_row)
    m_last = big_m[chunk - 1:chunk, :]
    m_sc[0:1, :] = b[chunk - 1:chunk, :] + m_last
    return dict(
        a_t=rec.T,
        big_m=big_m,
        inter=scale * jnp.exp(m_row - big_m),
        floor=jnp.exp(-(b + big_m)),
        m_last=m_last,
        decay=jnp.exp(m_row - m_last),
        k_t=k.astype(F32).T,
        causal=_causal_mask(chunk),
        ones_col=(lax.broadcasted_iota(jnp.int32, (chunk, LANES), 1) == 0).astype(BF16),
    )


def _mlstm_head(h, c, qh, vh, og, gain, ct_sc, between):
    scale = A_DQK ** -0.5
    a_row = c["a_t"][h:h + 1, :]
    k_t = c["k_t"][h * A_DQK:(h + 1) * A_DQK, :]
    qk = _dot(qh, k_t.astype(BF16))
    v_ext = jnp.concatenate([vh, c["ones_col"]], axis=1)
    state = ct_sc[h]
    w_row = jnp.exp(a_row - c["m_last"][:, h:h + 1])
    kw_t = (k_t * w_row).astype(BF16)
    ct_sc[h] = c["decay"][:, h:h + 1] * state + _dot(kw_t, v_ext)
    between()
    big_m = c["big_m"][:, h:h + 1]
    decay_mat = jnp.exp(jnp.where(c["causal"], (a_row + math.log(scale)) - big_m, -jnp.inf))
    p = (qk * decay_mat).astype(BF16)
    q_inter = (qh.astype(F32) * c["inter"][:, h:h + 1]).astype(BF16)
    tot = _dot(jnp.concatenate([p, q_inter], axis=1),
               jnp.concatenate([v_ext, state.astype(BF16)], axis=0))
    num = tot[:, :A_DV]
    den = tot[:, A_DV:A_DV + 1]
    inv = 1.0 / jnp.maximum(jnp.abs(den), c["floor"][:, h:h + 1])
    ms = jnp.mean(num * num, axis=1, keepdims=True)
    fac = inv * lax.rsqrt(inv * inv * ms + EPS)
    return (num * fac * gain * og.astype(F32)).astype(BF16)


A_PROJ_W = A_COL_XQ
A_UNIT_W = MXU_DIM


def _mlstm_layer_kernel(blocks_per_row, x_ref, g1_ref, win_ref, wout_ref, g_ref, w_ref, wg_ref, gb_ref,
                        gq_ref, gk_ref, hg_ref, x1_ref, xq_ref, y_ref, proj_sc, rec_sc, ct_sc, m_sc):
    s = pl.program_id(0)

    @pl.when(s == 0)
    def _():
        proj_sc[...] = jnp.zeros_like(proj_sc)
        rec_sc[...] = jnp.zeros_like(rec_sc)

    @pl.when(lax.rem(jnp.maximum(s - 1, 0), blocks_per_row) == 0)
    def _():
        ct_sc[...] = jnp.zeros_like(ct_sc)
        m_sc[...] = jnp.zeros_like(m_sc)

    put = proj_sc.at[lax.rem(s, 2)]
    put_rec = rec_sc.at[lax.rem(s, 2)]
    get = proj_sc.at[lax.rem(s + 1, 2)]
    get_rec = rec_sc.at[lax.rem(s + 1, 2)]

    ffn_units, ffn_result = _ffn_stream(x_ref[...], g1_ref, win_ref, wout_ref)
    xn = []

    def project(c):
        cols = slice(c * A_UNIT_W, (c + 1) * A_UNIT_W)
        y = _dot(xn[0], w_ref[:, cols])
        if cols.start >= A_COL_XQ:
            _store_mem_queries(y, gq_ref, gk_ref, xq_ref, cols.start - A_COL_XQ)
            return
        if cols.start >= A_COL_O:
            y = jax.nn.sigmoid(y)
        put[:, cols] = y.astype(BF16)

    chunks = [slice(c * MLSTM_CHUNK, (c + 1) * MLSTM_CHUNK) for c in range(ROW_TILE // MLSTM_CHUNK)]
    setups = {}

    def recur(c, h, between):
        rows = chunks[c]
        if h == 0:
            setups[c] = _mlstm_chunk_setup(get_rec[rows, :], get[rows, A_COL_K:A_COL_K + A_QK], m_sc)
        qk = slice(h * A_DQK, (h + 1) * A_DQK)
        v = slice(h * A_DV, (h + 1) * A_DV)
        y_ref[rows, v] = _mlstm_head(
            h, setups[c], get[rows, A_COL_Q + qk.start:A_COL_Q + qk.stop],
            get[rows, A_COL_V + v.start:A_COL_V + v.stop],
            get[rows, A_COL_O + v.start:A_COL_O + v.stop], hg_ref[:, v], ct_sc, between)

    def start_projection():
        x1 = ffn_result()
        x1_ref[...] = x1
        xn.append(_rms(x1, g_ref[...]).astype(BF16))
        gates = _dot(xn[0], wg_ref[...]) + gb_ref[...]
        project(0)
        project(1)
        for rows in chunks:
            put_rec[rows, :] = _mlstm_gate_record(gates[rows])

    stream = ffn_units + [start_projection] + [functools.partial(project, u) for u in range(2, A_MAIN // A_UNIT_W)]
    heads = [(c, h) for c in range(len(chunks)) for h in range(A_HEADS)]
    for (c, h), size in zip(heads, _shares(len(stream), len(heads))):
        share, stream = stream[:size], stream[size:]
        recur(c, h, lambda share=share: [unit() for unit in share])


def _mlstm_layer(x, ffn_g, ffn_in, ffn_out, mix_g, w_main, w_gates, gate_b, gq, gk, h_gain, layer, j, bsz, seq):
    n = bsz * seq
    nb = n // ROW_TILE
    last = nb - 1
    return pl.pallas_call(
        functools.partial(_mlstm_layer_kernel, seq // ROW_TILE),
        out_shape=(jax.ShapeDtypeStruct((n, D_MODEL), F32),
                   jax.ShapeDtypeStruct((n, M_Q), BF16),
                   jax.ShapeDtypeStruct((n, A_V), BF16)),
        grid=(nb + 1,),
        in_specs=[
            pl.BlockSpec((ROW_TILE, D_MODEL), lambda s: (jnp.minimum(s, last), 0)),
            _resident((None, 1, D_MODEL), lambda s: (layer, 0, 0)),
            _resident((None, D_MODEL, 2 * D_FF), lambda s: (layer, 0, 0)),
            _resident((None, D_FF, D_MODEL), lambda s: (layer, 0, 0)),
            _resident((None, 1, D_MODEL), lambda s: (layer, 0, 0)),
            _resident((None, D_MODEL, A_MAIN), lambda s: (j, 0, 0)),
            _resident((None, D_MODEL, LANES), lambda s: (j, 0, 0)),
            _resident((None, 1, LANES), lambda s: (j, 0, 0)),
            _resident((None, 1, M_HD), lambda s: (layer, 0, 0)),
            _resident((None, 1, M_HD), lambda s: (layer, 0, 0)),
            _resident((None, 1, A_V), lambda s: (j, 0, 0)),
        ],
        out_specs=(pl.BlockSpec((ROW_TILE, D_MODEL), lambda s: (jnp.minimum(s, last), 0)),
                   pl.BlockSpec((ROW_TILE, M_Q), lambda s: (jnp.minimum(s, last), 0)),
                   pl.BlockSpec((ROW_TILE, A_V), lambda s: (jnp.maximum(s - 1, 0), 0))),
        scratch_shapes=[pltpu.VMEM((2, ROW_TILE, A_PROJ_W), BF16),
                        pltpu.VMEM((2, ROW_TILE, LANES), F32),
                        pltpu.VMEM((A_HEADS, A_DQK, A_STATE_W), F32),
                        pltpu.VMEM((SUBLANES, LANES), F32)],
        compiler_params=_params("arbitrary"),
        name="mlstm_layer",
    )(x, ffn_g, ffn_in, ffn_out, mix_g, w_main, w_gates, gate_b, gq, gk, h_gain)


SWA_SUB = B_HD
SWA_SPAN = B_WINDOW + SWA_SUB


def _swa_units(sink_ref, q_ref, kv, first_block, y_ref):
    sub_q, span_k = SWA_SUB, SWA_SPAN
    pairs = B_GROUP // 2
    keys = kv[:, :B_KV]
    vals = kv[:, B_KV:]
    lane_kv = lax.broadcasted_iota(jnp.int32, keys.shape, 1)
    lane_q = lax.broadcasted_iota(jnp.int32, (sub_q, LANES), 1)
    kk = lax.broadcasted_iota(jnp.int32, (span_k, LANES), 0)
    qq = lax.broadcasted_iota(jnp.int32, (span_k, LANES), 1) % sub_q
    dist = B_WINDOW + qq - kk
    in_window = (dist >= 0) & (dist < B_WINDOW)
    odd_lane = lax.broadcasted_iota(jnp.int32, (1, LANES), 1) >= sub_q

    def own_half(x, grp):
        return jnp.where((lane_kv // B_HD) == grp, x, 0.0)

    def in_half(x, grp, half):
        own = own_half(x, grp)
        return own if grp == half else pltpu.roll(own, B_HD, axis=1)

    kg = [[in_half(keys, grp, half).astype(BF16) for half in range(2)] for grp in range(B_KV_HEADS)]
    vg_t = [(own_half(vals, grp) + pltpu.roll(own_half(vals, grp), B_HD, axis=1)).T.astype(BF16)
            for grp in range(B_KV_HEADS)]

    def tile_heads(grp, i):
        parity, first_pair = divmod(i, pairs // 2)
        first_pair *= 2
        return [grp * B_GROUP + 2 * (first_pair + k) + parity for k in range(2)]

    n_tiles = B_GROUP * sub_q // LANES
    sink_rows = [[jnp.where(odd_lane, sink_ref[hb], sink_ref[ha]) * LOG2E
                  for ha, hb in [tile_heads(grp, i) for i in range(n_tiles)]] for grp in range(B_KV_HEADS)]

    def scores_of(grp, t):
        stack = jnp.concatenate(
            [q_ref[t * sub_q:(t + 1) * sub_q, (grp * pairs + pp) * LANES:(grp * pairs + pp + 1) * LANES]
             for pp in range(pairs)], axis=0)
        span = slice(t * sub_q, t * sub_q + span_k)
        return [_dot_nt(kg[grp][half][span], stack) for half in range(2)]

    units = [(grp, t) for grp in range(B_KV_HEADS) for t in range(ROW_TILE // sub_q)]
    ahead = [scores_of(*units[0])]

    def unit(u, between):
        grp, t = units[u]
        scores_t = ahead[0]
        if u + 1 < len(units):
            ahead[0] = scores_of(*units[u + 1])
        between()
        rows = slice(t * sub_q, (t + 1) * sub_q)
        span = slice(t * sub_q, t * sub_q + span_k)
        valid = in_window
        if t * sub_q < B_WINDOW:
            valid = valid & ((kk >= B_WINDOW - t * sub_q) | jnp.logical_not(first_block))
        tiles = [sc[:, k * LANES:(k + 1) * LANES] for sc in scores_t for k in range(n_tiles // 2)]
        probs = []
        inv = []
        for i, tile in enumerate(tiles):
            sink = sink_rows[grp][i]
            s = jnp.where(valid, tile, -jnp.inf)
            m = jnp.maximum(jnp.max(s, axis=0, keepdims=True), sink)
            p = jnp.exp2(s - m)
            inv.append(1.0 / (jnp.sum(p, axis=0, keepdims=True) + jnp.exp2(sink - m)))
            probs.append(p.astype(BF16))
        out_t = _dot(vg_t[grp][:, span], jnp.concatenate(probs, axis=1))
        o = [(out_t[:, i * LANES:(i + 1) * LANES] * inv[i]).T for i in range(n_tiles)]
        for pp in range(pairs):
            even, odd = o[pp // 2], o[n_tiles // 2 + pp // 2]
            r0 = (pp % 2) * sub_q
            c0 = (grp * pairs + pp) * LANES
            y_ref[rows, c0:c0 + LANES] = jnp.where(lane_q < B_HD, even[r0:r0 + sub_q],
                                                   odd[r0:r0 + sub_q]).astype(BF16)

    return [functools.partial(unit, u) for u in range(len(units))]


B_PROJ_W = B_Q + 2 * B_KV


def _swa_layer_kernel(blocks_per_row, sink_ref, x_ref, pos_ref, g1_ref, win_ref, wout_ref, g_ref, w_ref,
                      gqk_ref, tab_ref, gq_ref, gk_ref, x1_ref, xq_ref, y_ref, proj_sc):
    s = pl.program_id(0)

    @pl.when(s == 0)
    def _():
        proj_sc[...] = jnp.zeros_like(proj_sc)

    put = proj_sc.at[lax.rem(s, 3)]
    get = proj_sc.at[lax.rem(s + 2, 3)]
    before = proj_sc.at[lax.rem(s + 1, 3)]
    first_block = lax.rem(jnp.maximum(s - 1, 0), blocks_per_row) == 0

    kv = jnp.concatenate([before[ROW_TILE - B_WINDOW:, B_Q:B_PROJ_W], get[:, B_Q:B_PROJ_W]], axis=0).astype(F32)
    attend = _swa_units(sink_ref, get, kv, first_block, y_ref)

    ffn_units, ffn_result = _ffn_stream(x_ref[...], g1_ref, win_ref, wout_ref)
    norm_rope, tables = _rope_projector(pos_ref, gqk_ref, tab_ref, put)
    cos_q, sin_q, cos_k, sin_k = tables
    n_q = B_Q // MXU_DIM
    cols = [s_ * MXU_DIM for s_ in range(n_q)] + [B_COL_KV, B_COL_XQ, B_COL_XQ + MXU_DIM]
    xn = []
    ahead = []

    def start_projection():
        x1 = ffn_result()
        x1_ref[...] = x1
        xn.append(_rms(x1, g_ref[...]).astype(BF16))
        ahead.append(_dot(xn[0], w_ref[:, cols[0]:cols[0] + MXU_DIM]))

    def project(i):
        slab = ahead[0]
        if i + 1 < len(cols):
            ahead[0] = _dot(xn[0], w_ref[:, cols[i + 1]:cols[i + 1] + MXU_DIM])
        if i < n_q:
            norm_rope(slab, cos_q, sin_q, i * MXU_DIM)
        elif i == n_q:
            v = slab[:, B_KV:]
            norm_rope(slab, cos_k, sin_k, B_Q)
            put[:, B_Q + B_KV:B_PROJ_W] = v.astype(BF16)
        else:
            _store_mem_queries(slab, gq_ref, gk_ref, xq_ref, cols[i] - B_COL_XQ)

    stream = ffn_units + [start_projection] + [functools.partial(project, i) for i in range(len(cols))]
    for unit, size in zip(attend, _shares(len(stream), len(attend))):
        share, stream = stream[:size], stream[size:]
        unit(lambda share=share: [item() for item in share])


def _swa_layer(x, pos, sinks, ffn_g, ffn_in, ffn_out, mix_g, w_main, gqk, rope_tab, gq, gk, layer, j, bsz, seq):
    n = bsz * seq
    nb = n // ROW_TILE
    last = nb - 1
    return pl.pallas_call(
        functools.partial(_swa_layer_kernel, seq // ROW_TILE),
        out_shape=(jax.ShapeDtypeStruct((n, D_MODEL), F32),
                   jax.ShapeDtypeStruct((n, M_Q), BF16),
                   jax.ShapeDtypeStruct((n, B_Q), BF16)),
        grid=(nb + 1,),
        in_specs=[
            pl.BlockSpec(memory_space=pltpu.SMEM),
            pl.BlockSpec((ROW_TILE, D_MODEL), lambda s: (jnp.minimum(s, last), 0)),
            pl.BlockSpec((None, 1, ROW_TILE), lambda s: (jnp.minimum(s, last), 0, 0)),
            _resident((None, 1, D_MODEL), lambda s: (layer, 0, 0)),
            _resident((None, D_MODEL, 2 * D_FF), lambda s: (layer, 0, 0)),
            _resident((None, D_FF, D_MODEL), lambda s: (layer, 0, 0)),
            _resident((None, 1, D_MODEL), lambda s: (layer, 0, 0)),
            _resident((None, D_MODEL, B_MAIN), lambda s: (j, 0, 0)),
            _resident((None, 4, LANES), lambda s: (j, 0, 0)),
            _resident((ROT_DIM, 2 * LANES), lambda s: (0, 0)),
            _resident((None, 1, M_HD), lambda s: (layer, 0, 0)),
            _resident((None, 1, M_HD), lambda s: (layer, 0, 0)),
        ],
        out_specs=(pl.BlockSpec((ROW_TILE, D_MODEL), lambda s: (jnp.minimum(s, last), 0)),
                   pl.BlockSpec((ROW_TILE, M_Q), lambda s: (jnp.minimum(s, last), 0)),
                   pl.BlockSpec((ROW_TILE, B_Q), lambda s: (jnp.maximum(s - 1, 0), 0))),
        scratch_shapes=[pltpu.VMEM((3, ROW_TILE, B_PROJ_W), BF16)],
        compiler_params=_params("arbitrary"),
        name="swa_layer",
    )(sinks, x, pos, ffn_g, ffn_in, ffn_out, mix_g, w_main, gqk, rope_tab, gq, gk)


def _out_ffn_kernel(x_ref, yt_ref, xq_ref, mkt_ref, mv_ref, wo_ref, g_ref, win_ref, wout_ref, o_ref):
    y_dim = yt_ref.shape[1]
    head_cols = [slice(h * M_HD, (h + 1) * M_HD) for h in range(M_HEADS)]
    parts = [slice(r * ROW_TILE, (r + 1) * ROW_TILE) for r in range(OUT_ROWS // ROW_TILE)]

    def mix_start(rows):
        scores = [_dot(xq_ref[rows, sl], mkt_ref[sl, :]) for sl in head_cols]
        return scores, x_ref[rows, :] + _dot(yt_ref[rows, :], wo_ref[:y_dim, :])

    def mix_finish(scores, acc):
        heads = []
        for s, sl in zip(scores, head_cols):
            p = jnp.exp(s - jnp.max(s, axis=1, keepdims=True))
            inv = 1.0 / jnp.sum(p, axis=1, keepdims=True)
            heads.append((_dot(p.astype(BF16), mv_ref[:, sl]) * inv).astype(BF16))
        return acc + _dot(jnp.concatenate(heads, axis=1), wo_ref[y_dim:, :])

    def ffn(x_mid):
        return _ffn_stream(x_mid, g_ref, win_ref, wout_ref)

    units, result = ffn(mix_finish(*mix_start(parts[0])))
    third = len(units) // 3
    for r in range(1, len(parts)):
        for unit in units[:third]:
            unit()
        started = mix_start(parts[r])
        for unit in units[third:2 * third]:
            unit()
        nxt_units, nxt_result = ffn(mix_finish(*started))
        for unit in units[2 * third:]:
            unit()
        o_ref[parts[r - 1], :] = result()
        units, result = nxt_units, nxt_result
    for unit in units:
        unit()
    o_ref[parts[-1], :] = result()


def _out_ffn(x, y_tok, xq, mem_k, mem_v, w_out, gains, w_in, w_ffn_out, layer, j, bsz, seq):
    nt = seq // OUT_ROWS
    y_dim = y_tok.shape[1]
    return pl.pallas_call(
        _out_ffn_kernel,
        out_shape=jax.ShapeDtypeStruct((bsz * seq, D_MODEL), F32),
        grid=(bsz, nt),
        in_specs=[
            pl.BlockSpec((OUT_ROWS, D_MODEL), lambda b, t: (b * nt + t, 0)),
            pl.BlockSpec((OUT_ROWS, y_dim), lambda b, t: (b * nt + t, 0)),
            pl.BlockSpec((OUT_ROWS, M_Q), lambda b, t: (b * nt + t, 0)),
            pl.BlockSpec((None, M_Q, N_MEM), lambda b, t: (b, 0, 0)),
            pl.BlockSpec((None, N_MEM, M_Q), lambda b, t: (b, 0, 0)),
            _resident((None, y_dim + M_Q, D_MODEL), lambda b, t: (j, 0, 0)),
            _resident((None, 1, D_MODEL), lambda b, t: (layer, 0, 0)),
            _resident((None, D_MODEL, 2 * D_FF), lambda b, t: (layer, 0, 0)),
            _resident((None, D_FF, D_MODEL), lambda b, t: (layer, 0, 0)),
        ],
        out_specs=pl.BlockSpec((OUT_ROWS, D_MODEL), lambda b, t: (b * nt + t, 0)),
        compiler_params=_params("parallel", "parallel"),
        name="out_ffn",
    )(x, y_tok, xq, mem_k, mem_v, w_out, gains, w_in, w_ffn_out)


def _rope_row_tables():
    inv_freq = ROPE_THETA ** (-np.arange(0, ROT_DIM, 2, dtype=np.float32) / ROT_DIM)
    tab = np.zeros((ROT_DIM, 2 * LANES), np.float32)
    tab[:, :LANES] = np.concatenate([inv_freq, inv_freq])[:, None]
    tab[:, LANES:] = np.where(np.arange(ROT_DIM) < ROT_HALF, -1.0, 1.0)[:, None]
    return jnp.asarray(tab)


def _rope_gain_rows(gain):
    d = np.arange(B_HD)
    partner = np.where(d < ROT_HALF, d + ROT_HALF, np.where(d < ROT_DIM, d - ROT_HALF, d))
    reps = LANES // B_HD
    return jnp.stack([jnp.tile(gain, (1, reps)), jnp.tile(gain[:, partner], (1, reps))], axis=1)


def kernel(x, mem, positions, mem_norm_g, mem_w_kv, ffn1_norm_g, ffn1_w_in, ffn1_w_out,
           mix_norm_g, ffn2_norm_g, ffn2_w_in, ffn2_w_out, xa_q_norm_g, xa_k_norm_g,
           a_w_in, a_gate_b, a_h_norm_g, a_w_out,
           b_w_in, b_q_norm_g, b_k_norm_g, b_sinks, b_w_out):
    bsz, seq, _ = x.shape
    n = bsz * seq
    assert seq % OUT_ROWS == 0 and OUT_ROWS % ROW_TILE == 0
    assert ROW_TILE % MLSTM_CHUNK == 0 and ROW_TILE % SWA_SUB == 0

    bf = lambda w: w.astype(BF16)
    row3 = lambda g: g.reshape(g.shape[0], 1, g.shape[1])
    ffn1_in, ffn1_out, ffn2_in, ffn2_out = bf(ffn1_w_in), bf(ffn1_w_out), bf(ffn2_w_in), bf(ffn2_w_out)
    a_main = bf(jnp.concatenate([a_w_in[..., :A_COL_XQ], a_w_in[..., A_TOK:]], axis=-1))
    a_gw = bf(jnp.pad(a_w_in[..., A_COL_XQ:A_TOK], ((0, 0), (0, 0), (0, LANES - A_GATES))))
    a_gb = row3(jnp.pad(a_gate_b, ((0, 0), (0, LANES - A_GATES))))
    b_main = bf(jnp.concatenate([b_w_in[..., :B_Q], b_w_in[..., B_TOK:], b_w_in[..., B_Q:B_TOK]], axis=-1))
    b_gqk = jnp.concatenate([_rope_gain_rows(b_q_norm_g), _rope_gain_rows(b_k_norm_g)], axis=1)
    a_out, b_out = bf(a_w_out), bf(b_w_out)
    rope_tab = _rope_row_tables()
    pos = positions.reshape(n // ROW_TILE, 1, ROW_TILE)
    g_ffn1, g_mix, g_ffn2 = row3(ffn1_norm_g), row3(mix_norm_g), row3(ffn2_norm_g)
    g_xq, g_xk, g_h = row3(xa_q_norm_g), row3(xa_k_norm_g), row3(a_h_norm_g)

    mem_k, mem_v = _memkv(mem, mem_norm_g.reshape(1, D_MODEL), bf(mem_w_kv))
    xf = x.reshape(n, D_MODEL)
    for i in range(DEPTH):
        j = i // 2
        if i % 2 == 0:
            xf, xq, y_tok = _mlstm_layer(xf, g_ffn1, ffn1_in, ffn1_out, g_mix, a_main, a_gw, a_gb,
                                         g_xq, g_xk, g_h, i, j, bsz, seq)
            w_out = a_out
        else:
            xf, xq, y_tok = _swa_layer(xf, pos, b_sinks[j], g_ffn1, ffn1_in, ffn1_out, g_mix, b_main, b_gqk,
                                       rope_tab, g_xq, g_xk, i, j, bsz, seq)
            w_out = b_out
        xf = _out_ffn(xf, y_tok, xq, mem_k, mem_v, w_out, g_ffn2, ffn2_in, ffn2_out, i, j, bsz, seq)
    return xf.reshape(bsz, seq, D_MODEL)
```

```python
import functools
import math

import numpy as np
import jax
import jax.numpy as jnp
from jax import lax
from jax.experimental import pallas as pl
from jax.experimental.pallas import tpu as pltpu

F32 = jnp.float32
BF16 = jnp.bfloat16

D_MODEL = 1024
DEPTH = 4
EPS = 1e-6
D_FF = 2816

A_HEADS = 4
A_DQK = 128
A_DV = 256
A_QK = A_HEADS * A_DQK
A_V = A_HEADS * A_DV
A_GATES = 2 * A_HEADS
A_TOK = 2 * A_QK + 2 * A_V + A_GATES

B_HEADS = 16
B_KV_HEADS = 2
B_GROUP = B_HEADS // B_KV_HEADS
B_HD = 64
B_WINDOW = 128
B_Q = B_HEADS * B_HD
B_KV = B_KV_HEADS * B_HD
B_TOK = B_Q + 2 * B_KV
ROT_DIM = 16
ROT_HALF = ROT_DIM // 2
ROPE_THETA = 500000.0

N_MEM = 256
M_HEADS = 4
M_HD = 128
M_Q = M_HEADS * M_HD

LOG2E = math.log2(math.e)

LANES = 128
SUBLANES = 8
MXU_DIM = 256
VMEM_LIMIT = 56 * 1024 * 1024

ROW_TILE = 512
FF_TILE = MXU_DIM
OUT_ROWS = 1024
MLSTM_CHUNK = 256

A_COL_Q, A_COL_K, A_COL_V, A_COL_O = 0, A_QK, 2 * A_QK, 2 * A_QK + A_V
A_COL_XQ = 2 * A_QK + 2 * A_V
A_MAIN = A_COL_XQ + M_Q
G_LANE_A, G_LANE_B, G_LANE_CM = 0, A_HEADS, 2 * A_HEADS
B_COL_XQ = B_Q
B_COL_KV = B_Q + M_Q
B_MAIN = B_COL_KV + 2 * B_KV


def _params(*semantics):
    return pltpu.CompilerParams(dimension_semantics=semantics, vmem_limit_bytes=VMEM_LIMIT)


def _resident(block_shape, index_map):
    return pl.BlockSpec(block_shape, index_map, pipeline_mode=pl.Buffered(1))


def _rms(x, gain):
    ms = jnp.mean(x * x, axis=-1, keepdims=True)
    return x * lax.rsqrt(ms + EPS) * gain


def _dot(a, b):
    return jnp.dot(a, b, preferred_element_type=F32)


def _dot_nt(a, b):
    return lax.dot_general(a, b, (((1,), (1,)), ((), ())), preferred_element_type=F32)


def _split2(x):
    hi = x.astype(BF16)
    return hi, (x - hi.astype(F32)).astype(BF16)


def _ffn_stream(x, g_ref, win_ref, wout_ref):
    xn = _rms(x, g_ref[...]).astype(BF16)
    acc = [jnp.zeros_like(x)]

    def unit(j):
        lo = j * FF_TILE
        gate = _dot(xn, win_ref[:, lo:lo + FF_TILE])
        up = _dot(xn, win_ref[:, D_FF + lo:D_FF + lo + FF_TILE])
        h = (gate * jax.nn.sigmoid(gate) * up).astype(BF16)
        acc[0] = acc[0] + _dot(h, wout_ref[lo:lo + FF_TILE, :])

    return [functools.partial(unit, j) for j in range(D_FF // FF_TILE)], lambda: x + 0.5 * acc[0]


def _shares(n_items, n_takers):
    base, extra = divmod(n_items, n_takers)
    return [base + (i < extra) for i in range(n_takers)]


def _memkv_kernel(mem_ref, g_ref, w_ref, k_ref, v_ref):
    mn = _rms(mem_ref[...], g_ref[...]).astype(BF16)
    kv = _dot(mn, w_ref[...])
    for h in range(M_HEADS):
        kh = kv[:, h * M_HD:(h + 1) * M_HD]
        ms = jnp.mean(kh * kh, axis=-1, keepdims=True)
        k_ref[h * M_HD:(h + 1) * M_HD, :] = (kh * lax.rsqrt(ms + EPS)).T.astype(BF16)
    v_ref[...] = kv[:, M_Q:].astype(BF16)


def _memkv(mem, gain, w_kv):
    bsz = mem.shape[0]
    return pl.pallas_call(
        _memkv_kernel,
        out_shape=(jax.ShapeDtypeStruct((bsz, M_Q, N_MEM), BF16),
                   jax.ShapeDtypeStruct((bsz, N_MEM, M_Q), BF16)),
        grid=(bsz,),
        in_specs=[
            pl.BlockSpec((None, N_MEM, D_MODEL), lambda b: (b, 0, 0)),
            _resident((1, D_MODEL), lambda b: (0, 0)),
            _resident((D_MODEL, 2 * M_Q), lambda b: (0, 0)),
        ],
        out_specs=(pl.BlockSpec((None, M_Q, N_MEM), lambda b: (b, 0, 0)),
                   pl.BlockSpec((None, N_MEM, M_Q), lambda b: (b, 0, 0))),
        compiler_params=_params("parallel"),
        name="memkv",
    )(mem, gain, w_kv)


def _store_mem_queries(xq, gq_ref, gk_ref, out_ref, col0):
    geff = gq_ref[...] * gk_ref[...] * (M_HD ** -0.5)
    for h in range(xq.shape[1] // M_HD):
        s = xq[:, h * M_HD:(h + 1) * M_HD]
        ms = jnp.mean(s * s, axis=-1, keepdims=True)
        out_ref[:, col0 + h * M_HD:col0 + (h + 1) * M_HD] = (s * lax.rsqrt(ms + EPS) * geff).astype(BF16)


def _log_sigmoid(x):
    return jnp.minimum(x, 0.0) - jnp.log(1.0 + jnp.exp(-jnp.abs(x)))


def _causal_mask(n):
    row = lax.broadcasted_iota(jnp.int32, (n, n), 0)
    col = lax.broadcasted_iota(jnp.int32, (n, n), 1)
    return col <= row


def _mlstm_gate_record(g):
    chunk = g.shape[0]
    causal = _causal_mask(chunk)
    hi, lo = _split2(_log_sigmoid(g))
    both = _dot(causal.astype(BF16), jnp.concatenate([hi, lo], axis=1))
    b = both[:, :LANES] + both[:, LANES:]
    a = g - pltpu.roll(b, LANES - A_HEADS, axis=1)
    a_t = a.T
    lane = lax.broadcasted_iota(jnp.int32, (chunk, LANES), 1)
    rec = jnp.where(lane < G_LANE_B, a, jnp.where(lane < G_LANE_CM, b, 0.0))
    for h in range(A_HEADS):
        run_max = jnp.max(jnp.where(causal, a_t[h:h + 1, :], -jnp.inf), axis=1, keepdims=True)
        rec = jnp.where(lane == G_LANE_CM + h, run_max, rec)
    return rec


ROPE_PAD = B_HD - ROT_DIM


def _rope_projector(pos_ref, gqk_ref, tab_ref, out_ref):
    ang = tab_ref[:, 0:1] * pos_ref[...].astype(F32)
    cos_c = jnp.cos(ang)
    sin_c = jnp.sin(ang) * tab_ref[:, LANES:LANES + 1]
    pad = jnp.zeros((ROPE_PAD, ang.shape[1]), F32)
    cos_t = jnp.concatenate([cos_c, pad, cos_c, pad], axis=0).T
    sin_t = jnp.concatenate([sin_c, pad, sin_c, pad], axis=0).T
    lane = lax.broadcasted_iota(jnp.int32, (1, LANES), 1) % B_HD
    cos_t = cos_t + (lane >= ROT_DIM).astype(F32)
    first_half = lane < ROT_HALF
    r = lax.broadcasted_iota(jnp.int32, (MXU_DIM, MXU_DIM), 0) // B_HD
    c = lax.broadcasted_iota(jnp.int32, (MXU_DIM, MXU_DIM), 1) // B_HD
    seg = jnp.where(r == c, 1.0 / B_HD, 0.0).astype(BF16)

    def norm_rope(slab, cos_g, sin_g, col0):
        hi, lo = _split2(slab * slab)
        inv = lax.rsqrt(_dot(hi, seg) + _dot(lo, seg) + EPS)
        for half in range(MXU_DIM // LANES):
            sl = slice(half * LANES, (half + 1) * LANES)
            xh = slab[:, sl]
            partner = jnp.where(first_half, pltpu.roll(xh, LANES - ROT_HALF, axis=1),
                                pltpu.roll(xh, ROT_HALF, axis=1))
            out = (xh * cos_g + partner * sin_g) * inv[:, sl]
            out_ref[:, col0 + half * LANES:col0 + (half + 1) * LANES] = out.astype(BF16)

    q_scale = B_HD ** -0.5 * LOG2E
    cos_q, sin_q = cos_t * (gqk_ref[0:1, :] * q_scale), sin_t * (gqk_ref[1:2, :] * q_scale)
    cos_k, sin_k = cos_t * gqk_ref[2:3, :], sin_t * gqk_ref[3:4, :]
    return norm_rope, (cos_q, sin_q, cos_k, sin_k)


A_STATE_W = A_DV + LANES


def _mlstm_chunk_setup(rec, k, m_sc):
    chunk = rec.shape[0]
    scale = A_DQK ** -0.5
    b = pltpu.roll(rec, LANES - G_LANE_B, axis=1)
    run_max = pltpu.roll(rec, LANES - G_LANE_CM, axis=1)
    m_row = m_sc[0:1, :]
    big_m = jnp.maximum(run_max, m_row)
    m_last = big_m[chunk - 1:chunk, :]
    m_sc[0:1, :] = b[chunk - 1:chunk, :] + m_last
    return dict(
        a_t=rec.T,
        big_m=big_m,
        inter=scale * jnp.exp(m_row - big_m),
        floor=jnp.exp(-(b + big_m)),
        m_last=m_last,
        decay=jnp.exp(m_row - m_last),
        k_t=k.astype(F32).T,
        causal=_causal_mask(chunk),
        ones_col=(lax.broadcasted_iota(jnp.int32, (chunk, LANES), 1) == 0).astype(BF16),
    )


def _mlstm_head(h, c, qh, vh, og, gain, ct_sc, between):
    scale = A_DQK ** -0.5
    a_row = c["a_t"][h:h + 1, :]
    k_t = c["k_t"][h * A_DQK:(h + 1) * A_DQK, :]
    qk = _dot(qh, k_t.astype(BF16))
    v_ext = jnp.concatenate([vh, c["ones_col"]], axis=1)
    state = ct_sc[h]
    w_row = jnp.exp(a_row - c["m_last"][:, h:h + 1])
    kw_t = (k_t * w_row).astype(BF16)
    ct_sc[h] = c["decay"][:, h:h + 1] * state + _dot(kw_t, v_ext)
    between()
    big_m = c["big_m"][:, h:h + 1]
    decay_mat = jnp.exp(jnp.where(c["causal"], (a_row + math.log(scale)) - big_m, -jnp.inf))
    p = (qk * decay_mat).astype(BF16)
    q_inter = (qh.astype(F32) * c["inter"][:, h:h + 1]).astype(BF16)
    tot = _dot(jnp.concatenate([p, q_inter], axis=1),
               jnp.concatenate([v_ext, state.astype(BF16)], axis=0))
    num = tot[:, :A_DV]
    den = tot[:, A_DV:A_DV + 1]
    inv = 1.0 / jnp.maximum(jnp.abs(den), c["floor"][:, h:h + 1])
    ms = jnp.mean(num * num, axis=1, keepdims=True)
    fac = inv * lax.rsqrt(inv * inv * ms + EPS)
    return (num * fac * gain * og.astype(F32)).astype(BF16)


A_PROJ_W = A_COL_XQ
A_UNIT_W = MXU_DIM


def _mlstm_layer_kernel(blocks_per_row, x_ref, g1_ref, win_ref, wout_ref, g_ref, w_ref, wg_ref, gb_ref,
                        gq_ref, gk_ref, hg_ref, x1_ref, xq_ref, y_ref, proj_sc, rec_sc, ct_sc, m_sc):
    s = pl.program_id(0)

    @pl.when(lax.rem(jnp.maximum(s - 1, 0), blocks_per_row) == 0)
    def _():
        ct_sc[...] = jnp.zeros_like(ct_sc)
        m_sc[...] = jnp.zeros_like(m_sc)

    put = proj_sc.at[lax.rem(s, 2)]
    put_rec = rec_sc.at[lax.rem(s, 2)]
    get = proj_sc.at[lax.rem(s + 1, 2)]
    get_rec = rec_sc.at[lax.rem(s + 1, 2)]

    chunks = [slice(c * MLSTM_CHUNK, (c + 1) * MLSTM_CHUNK) for c in range(ROW_TILE // MLSTM_CHUNK)]

    def run(do_project, do_recur):
        xn = []
        setups = {}

        def project(c):
            cols = slice(c * A_UNIT_W, (c + 1) * A_UNIT_W)
            y = _dot(xn[0], w_ref[:, cols])
            if cols.start >= A_COL_XQ:
                _store_mem_queries(y, gq_ref, gk_ref, xq_ref, cols.start - A_COL_XQ)
                return
            if cols.start >= A_COL_O:
                y = jax.nn.sigmoid(y)
            put[:, cols] = y.astype(BF16)

        def recur(c, h, between):
            rows = chunks[c]
            if h == 0:
                setups[c] = _mlstm_chunk_setup(get_rec[rows, :], get[rows, A_COL_K:A_COL_K + A_QK], m_sc)
            qk = slice(h * A_DQK, (h + 1) * A_DQK)
            v = slice(h * A_DV, (h + 1) * A_DV)
            y_ref[rows, v] = _mlstm_head(
                h, setups[c], get[rows, A_COL_Q + qk.start:A_COL_Q + qk.stop],
                get[rows, A_COL_V + v.start:A_COL_V + v.stop],
                get[rows, A_COL_O + v.start:A_COL_O + v.stop], hg_ref[:, v], ct_sc, between)

        def start_projection():
            x1 = ffn_result()
            x1_ref[...] = x1
            xn.append(_rms(x1, g_ref[...]).astype(BF16))
            gates = _dot(xn[0], wg_ref[...]) + gb_ref[...]
            project(0)
            project(1)
            for rows in chunks:
                put_rec[rows, :] = _mlstm_gate_record(gates[rows])

        stream = []
        if do_project:
            ffn_units, ffn_result = _ffn_stream(x_ref[...], g1_ref, win_ref, wout_ref)
            stream = ffn_units + [start_projection] + [functools.partial(project, u)
                                                       for u in range(2, A_MAIN // A_UNIT_W)]
        if not do_recur:
            for unit in stream:
                unit()
            return
        heads = [(c, h) for c in range(len(chunks)) for h in range(A_HEADS)]
        for (c, h), size in zip(heads, _shares(len(stream), len(heads))):
            share, stream = stream[:size], stream[size:]
            recur(c, h, lambda share=share: [unit() for unit in share])

    last_step = pl.num_programs(0) - 1
    pl.when(s == 0)(lambda: run(True, False))
    pl.when((s > 0) & (s < last_step))(lambda: run(True, True))
    pl.when(s == last_step)(lambda: run(False, True))


def _mlstm_layer(x, ffn_g, ffn_in, ffn_out, mix_g, w_main, w_gates, gate_b, gq, gk, h_gain, layer, j, bsz, seq):
    n = bsz * seq
    nb = n // ROW_TILE
    last = nb - 1
    return pl.pallas_call(
        functools.partial(_mlstm_layer_kernel, seq // ROW_TILE),
        out_shape=(jax.ShapeDtypeStruct((n, D_MODEL), F32),
                   jax.ShapeDtypeStruct((n, M_Q), BF16),
                   jax.ShapeDtypeStruct((n, A_V), BF16)),
        grid=(nb + 1,),
        in_specs=[
            pl.BlockSpec((ROW_TILE, D_MODEL), lambda s: (jnp.minimum(s, last), 0)),
            _resident((None, 1, D_MODEL), lambda s: (layer, 0, 0)),
            _resident((None, D_MODEL, 2 * D_FF), lambda s: (layer, 0, 0)),
            _resident((None, D_FF, D_MODEL), lambda s: (layer, 0, 0)),
            _resident((None, 1, D_MODEL), lambda s: (layer, 0, 0)),
            _resident((None, D_MODEL, A_MAIN), lambda s: (j, 0, 0)),
            _resident((None, D_MODEL, LANES), lambda s: (j, 0, 0)),
            _resident((None, 1, LANES), lambda s: (j, 0, 0)),
            _resident((None, 1, M_HD), lambda s: (layer, 0, 0)),
            _resident((None, 1, M_HD), lambda s: (layer, 0, 0)),
            _resident((None, 1, A_V), lambda s: (j, 0, 0)),
        ],
        out_specs=(pl.BlockSpec((ROW_TILE, D_MODEL), lambda s: (jnp.minimum(s, last), 0)),
                   pl.BlockSpec((ROW_TILE, M_Q), lambda s: (jnp.minimum(s, last), 0)),
                   pl.BlockSpec((ROW_TILE, A_V), lambda s: (jnp.maximum(s - 1, 0), 0))),
        scratch_shapes=[pltpu.VMEM((2, ROW_TILE, A_PROJ_W), BF16),
                        pltpu.VMEM((2, ROW_TILE, LANES), F32),
                        pltpu.VMEM((A_HEADS, A_DQK, A_STATE_W), F32),
                        pltpu.VMEM((SUBLANES, LANES), F32)],
        compiler_params=_params("arbitrary"),
        name="mlstm_layer",
    )(x, ffn_g, ffn_in, ffn_out, mix_g, w_main, w_gates, gate_b, gq, gk, h_gain)


SWA_SUB = B_HD
SWA_SPAN = B_WINDOW + SWA_SUB


def _swa_units(sink_ref, q_ref, kv, first_block, y_ref):
    sub_q, span_k = SWA_SUB, SWA_SPAN
    pairs = B_GROUP // 2
    keys = kv[:, :B_KV]
    vals = kv[:, B_KV:]
    lane_kv = lax.broadcasted_iota(jnp.int32, keys.shape, 1)
    lane_q = lax.broadcasted_iota(jnp.int32, (sub_q, LANES), 1)
    kk = lax.broadcasted_iota(jnp.int32, (span_k, LANES), 0)
    qq = lax.broadcasted_iota(jnp.int32, (span_k, LANES), 1) % sub_q
    dist = B_WINDOW + qq - kk
    in_window = (dist >= 0) & (dist < B_WINDOW)
    odd_lane = lax.broadcasted_iota(jnp.int32, (1, LANES), 1) >= sub_q

    def own_half(x, grp):
        return jnp.where((lane_kv // B_HD) == grp, x, 0.0)

    def in_half(x, grp, half):
        own = own_half(x, grp)
        return own if grp == half else pltpu.roll(own, B_HD, axis=1)

    kg = [[in_half(keys, grp, half).astype(BF16) for half in range(2)] for grp in range(B_KV_HEADS)]
    vg_t = [(own_half(vals, grp) + pltpu.roll(own_half(vals, grp), B_HD, axis=1)).T.astype(BF16)
            for grp in range(B_KV_HEADS)]

    def tile_heads(grp, i):
        parity, first_pair = divmod(i, pairs // 2)
        first_pair *= 2
        return [grp * B_GROUP + 2 * (first_pair + k) + parity for k in range(2)]

    n_tiles = B_GROUP * sub_q // LANES
    sink_rows = [[jnp.where(odd_lane, sink_ref[hb], sink_ref[ha]) * LOG2E
                  for ha, hb in [tile_heads(grp, i) for i in range(n_tiles)]] for grp in range(B_KV_HEADS)]

    def scores_of(grp, t):
        stack = jnp.concatenate(
            [q_ref[t * sub_q:(t + 1) * sub_q, (grp * pairs + pp) * LANES:(grp * pairs + pp + 1) * LANES]
             for pp in range(pairs)], axis=0)
        span = slice(t * sub_q, t * sub_q + span_k)
        return [_dot_nt(kg[grp][half][span], stack) for half in range(2)]

    units = [(grp, t) for grp in range(B_KV_HEADS) for t in range(ROW_TILE // sub_q)]
    ahead = [scores_of(*units[0])]

    def unit(u, between):
        grp, t = units[u]
        scores_t = ahead[0]
        if u + 1 < len(units):
            ahead[0] = scores_of(*units[u + 1])
        between()
        rows = slice(t * sub_q, (t + 1) * sub_q)
        span = slice(t * sub_q, t * sub_q + span_k)
        valid = in_window
        if t * sub_q < B_WINDOW:
            valid = valid & ((kk >= B_WINDOW - t * sub_q) | jnp.logical_not(first_block))
        tiles = [sc[:, k * LANES:(k + 1) * LANES] for sc in scores_t for k in range(n_tiles // 2)]
        probs = []
        inv = []
        for i, tile in enumerate(tiles):
            sink = sink_rows[grp][i]
            s = jnp.where(valid, tile, -jnp.inf)
            m = jnp.maximum(jnp.max(s, axis=0, keepdims=True), sink)
            p = jnp.exp2(s - m)
            inv.append(1.0 / (jnp.sum(p, axis=0, keepdims=True) + jnp.exp2(sink - m)))
            probs.append(p.astype(BF16))
        out_t = _dot(vg_t[grp][:, span], jnp.concatenate(probs, axis=1))
        o = [(out_t[:, i * LANES:(i + 1) * LANES] * inv[i]).T for i in range(n_tiles)]
        for pp in range(pairs):
            even, odd = o[pp // 2], o[n_tiles // 2 + pp // 2]
            r0 = (pp % 2) * sub_q
            c0 = (grp * pairs + pp) * LANES
            y_ref[rows, c0:c0 + LANES] = jnp.where(lane_q < B_HD, even[r0:r0 + sub_q],
                                                   odd[r0:r0 + sub_q]).astype(BF16)

    return [functools.partial(unit, u) for u in range(len(units))]


B_PROJ_W = B_Q + 2 * B_KV


def _swa_layer_kernel(blocks_per_row, sink_ref, x_ref, pos_ref, g1_ref, win_ref, wout_ref, g_ref, w_ref,
                      gqk_ref, tab_ref, gq_ref, gk_ref, x1_ref, xq_ref, y_ref, proj_sc):
    s = pl.program_id(0)

    @pl.when(s == 0)
    def _():
        proj_sc[...] = jnp.zeros_like(proj_sc)

    put = proj_sc.at[lax.rem(s, 3)]
    get = proj_sc.at[lax.rem(s + 2, 3)]
    before = proj_sc.at[lax.rem(s + 1, 3)]
    first_block = lax.rem(jnp.maximum(s - 1, 0), blocks_per_row) == 0

    n_q = B_Q // MXU_DIM
    cols = [s_ * MXU_DIM for s_ in range(n_q)] + [B_COL_KV, B_COL_XQ, B_COL_XQ + MXU_DIM]

    def run(do_project, do_attend):
        xn = []
        ahead = []

        def start_projection():
            x1 = ffn_result()
            x1_ref[...] = x1
            xn.append(_rms(x1, g_ref[...]).astype(BF16))
            ahead.append(_dot(xn[0], w_ref[:, cols[0]:cols[0] + MXU_DIM]))

        def project(i):
            slab = ahead[0]
            if i + 1 < len(cols):
                ahead[0] = _dot(xn[0], w_ref[:, cols[i + 1]:cols[i + 1] + MXU_DIM])
            if i < n_q:
                norm_rope(slab, cos_q, sin_q, i * MXU_DIM)
            elif i == n_q:
                v = slab[:, B_KV:]
                norm_rope(slab, cos_k, sin_k, B_Q)
                put[:, B_Q + B_KV:B_PROJ_W] = v.astype(BF16)
            else:
                _store_mem_queries(slab, gq_ref, gk_ref, xq_ref, cols[i] - B_COL_XQ)

        attend = []
        if do_attend:
            kv = jnp.concatenate([before[ROW_TILE - B_WINDOW:, B_Q:B_PROJ_W], get[:, B_Q:B_PROJ_W]],
                                 axis=0).astype(F32)
            attend = _swa_units(sink_ref, get, kv, first_block, y_ref)
        stream = []
        if do_project:
            ffn_units, ffn_result = _ffn_stream(x_ref[...], g1_ref, win_ref, wout_ref)
            norm_rope, (cos_q, sin_q, cos_k, sin_k) = _rope_projector(pos_ref, gqk_ref, tab_ref, put)
            stream = ffn_units + [start_projection] + [functools.partial(project, i) for i in range(len(cols))]
        if not do_attend:
            for item in stream:
                item()
            return
        for unit, size in zip(attend, _shares(len(stream), len(attend))):
            share, stream = stream[:size], stream[size:]
            unit(lambda share=share: [item() for item in share])

    last_step = pl.num_programs(0) - 1
    pl.when(s == 0)(lambda: run(True, False))
    pl.when((s > 0) & (s < last_step))(lambda: run(True, True))
    pl.when(s == last_step)(lambda: run(False, True))


def _swa_layer(x, pos, sinks, ffn_g, ffn_in, ffn_out, mix_g, w_main, gqk, rope_tab, gq, gk, layer, j, bsz, seq):
    n = bsz * seq
    nb = n // ROW_TILE
    last = nb - 1
    return pl.pallas_call(
        functools.partial(_swa_layer_kernel, seq // ROW_TILE),
        out_shape=(jax.ShapeDtypeStruct((n, D_MODEL), F32),
                   jax.ShapeDtypeStruct((n, M_Q), BF16),
                   jax.ShapeDtypeStruct((n, B_Q), BF16)),
        grid=(nb + 1,),
        in_specs=[
            pl.BlockSpec(memory_space=pltpu.SMEM),
            pl.BlockSpec((ROW_TILE, D_MODEL), lambda s: (jnp.minimum(s, last), 0)),
            pl.BlockSpec((None, 1, ROW_TILE), lambda s: (jnp.minimum(s, last), 0, 0)),
            _resident((None, 1, D_MODEL), lambda s: (layer, 0, 0)),
            _resident((None, D_MODEL, 2 * D_FF), lambda s: (layer, 0, 0)),
            _resident((None, D_FF, D_MODEL), lambda s: (layer, 0, 0)),
            _resident((None, 1, D_MODEL), lambda s: (layer, 0, 0)),
            _resident((None, D_MODEL, B_MAIN), lambda s: (j, 0, 0)),
            _resident((None, 4, LANES), lambda s: (j, 0, 0)),
            _resident((ROT_DIM, 2 * LANES), lambda s: (0, 0)),
            _resident((None, 1, M_HD), lambda s: (layer, 0, 0)),
            _resident((None, 1, M_HD), lambda s: (layer, 0, 0)),
        ],
        out_specs=(pl.BlockSpec((ROW_TILE, D_MODEL), lambda s: (jnp.minimum(s, last), 0)),
                   pl.BlockSpec((ROW_TILE, M_Q), lambda s: (jnp.minimum(s, last), 0)),
                   pl.BlockSpec((ROW_TILE, B_Q), lambda s: (jnp.maximum(s - 1, 0), 0))),
        scratch_shapes=[pltpu.VMEM((3, ROW_TILE, B_PROJ_W), BF16)],
        compiler_params=_params("arbitrary"),
        name="swa_layer",
    )(sinks, x, pos, ffn_g, ffn_in, ffn_out, mix_g, w_main, gqk, rope_tab, gq, gk)


def _out_ffn_kernel(x_ref, yt_ref, xq_ref, mkt_ref, mv_ref, wo_ref, g_ref, win_ref, wout_ref, o_ref):
    y_dim = yt_ref.shape[1]
    head_cols = [slice(h * M_HD, (h + 1) * M_HD) for h in range(M_HEADS)]
    parts = [slice(r * ROW_TILE, (r + 1) * ROW_TILE) for r in range(OUT_ROWS // ROW_TILE)]

    def mix_start(rows):
        scores = [_dot(xq_ref[rows, sl], mkt_ref[sl, :]) for sl in head_cols]
        return scores, x_ref[rows, :] + _dot(yt_ref[rows, :], wo_ref[:y_dim, :])

    def mix_finish(scores, acc):
        heads = []
        for s, sl in zip(scores, head_cols):
            p = jnp.exp(s - jnp.max(s, axis=1, keepdims=True))
            inv = 1.0 / jnp.sum(p, axis=1, keepdims=True)
            heads.append((_dot(p.astype(BF16), mv_ref[:, sl]) * inv).astype(BF16))
        return acc + _dot(jnp.concatenate(heads, axis=1), wo_ref[y_dim:, :])

    def ffn(x_mid):
        return _ffn_stream(x_mid, g_ref, win_ref, wout_ref)

    units, result = ffn(mix_finish(*mix_start(parts[0])))
    third = len(units) // 3
    for r in range(1, len(parts)):
        for unit in units[:third]:
            unit()
        started = mix_start(parts[r])
        for unit in units[third:2 * third]:
            unit()
        nxt_units, nxt_result = ffn(mix_finish(*started))
        for unit in units[2 * third:]:
            unit()
        o_ref[parts[r - 1], :] = result()
        units, result = nxt_units, nxt_result
    for unit in units:
        unit()
    o_ref[parts[-1], :] = result()


def _out_ffn(x, y_tok, xq, mem_k, mem_v, w_out, gains, w_in, w_ffn_out, layer, j, bsz, seq):
    nt = seq // OUT_ROWS
    y_dim = y_tok.shape[1]
    return pl.pallas_call(
        _out_ffn_kernel,
        out_shape=jax.ShapeDtypeStruct((bsz * seq, D_MODEL), F32),
        grid=(bsz, nt),
        in_specs=[
            pl.BlockSpec((OUT_ROWS, D_MODEL), lambda b, t: (b * nt + t, 0)),
            pl.BlockSpec((OUT_ROWS, y_dim), lambda b, t: (b * nt + t, 0)),
            pl.BlockSpec((OUT_ROWS, M_Q), lambda b, t: (b * nt + t, 0)),
            pl.BlockSpec((None, M_Q, N_MEM), lambda b, t: (b, 0, 0)),
            pl.BlockSpec((None, N_MEM, M_Q), lambda b, t: (b, 0, 0)),
            _resident((None, y_dim + M_Q, D_MODEL), lambda b, t: (j, 0, 0)),
            _resident((None, 1, D_MODEL), lambda b, t: (layer, 0, 0)),
            _resident((None, D_MODEL, 2 * D_FF), lambda b, t: (layer, 0, 0)),
            _resident((None, D_FF, D_MODEL), lambda b, t: (layer, 0, 0)),
        ],
        out_specs=pl.BlockSpec((OUT_ROWS, D_MODEL), lambda b, t: (b * nt + t, 0)),
        compiler_params=_params("parallel", "parallel"),
        name="out_ffn",
    )(x, y_tok, xq, mem_k, mem_v, w_out, gains, w_in, w_ffn_out)


def _rope_row_tables():
    inv_freq = ROPE_THETA ** (-np.arange(0, ROT_DIM, 2, dtype=np.float32) / ROT_DIM)
    tab = np.zeros((ROT_DIM, 2 * LANES), np.float32)
    tab[:, :LANES] = np.concatenate([inv_freq, inv_freq])[:, None]
    tab[:, LANES:] = np.where(np.arange(ROT_DIM) < ROT_HALF, -1.0, 1.0)[:, None]
    return jnp.asarray(tab)


def _rope_gain_rows(gain):
    d = np.arange(B_HD)
    partner = np.where(d < ROT_HALF, d + ROT_HALF, np.where(d < ROT_DIM, d - ROT_HALF, d))
    reps = LANES // B_HD
    return jnp.stack([jnp.tile(gain, (1, reps)), jnp.tile(gain[:, partner], (1, reps))], axis=1)


def kernel(x, mem, positions, mem_norm_g, mem_w_kv, ffn1_norm_g, ffn1_w_in, ffn1_w_out,
           mix_norm_g, ffn2_norm_g, ffn2_w_in, ffn2_w_out, xa_q_norm_g, xa_k_norm_g,
           a_w_in, a_gate_b, a_h_norm_g, a_w_out,
           b_w_in, b_q_norm_g, b_k_norm_g, b_sinks, b_w_out):
    bsz, seq, _ = x.shape
    n = bsz * seq
    assert seq % OUT_ROWS == 0 and OUT_ROWS % ROW_TILE == 0
    assert ROW_TILE % MLSTM_CHUNK == 0 and ROW_TILE % SWA_SUB == 0

    bf = lambda w: w.astype(BF16)
    row3 = lambda g: g.reshape(g.shape[0], 1, g.shape[1])
    ffn1_in, ffn1_out, ffn2_in, ffn2_out = bf(ffn1_w_in), bf(ffn1_w_out), bf(ffn2_w_in), bf(ffn2_w_out)
    a_main = bf(jnp.concatenate([a_w_in[..., :A_COL_XQ], a_w_in[..., A_TOK:]], axis=-1))
    a_gw = bf(jnp.pad(a_w_in[..., A_COL_XQ:A_TOK], ((0, 0), (0, 0), (0, LANES - A_GATES))))
    a_gb = row3(jnp.pad(a_gate_b, ((0, 0), (0, LANES - A_GATES))))
    b_main = bf(jnp.concatenate([b_w_in[..., :B_Q], b_w_in[..., B_TOK:], b_w_in[..., B_Q:B_TOK]], axis=-1))
    b_gqk = jnp.concatenate([_rope_gain_rows(b_q_norm_g), _rope_gain_rows(b_k_norm_g)], axis=1)
    a_out, b_out = bf(a_w_out), bf(b_w_out)
    rope_tab = _rope_row_tables()
    pos = positions.reshape(n // ROW_TILE, 1, ROW_TILE)
    g_ffn1, g_mix, g_ffn2 = row3(ffn1_norm_g), row3(mix_norm_g), row3(ffn2_norm_g)
    g_xq, g_xk, g_h = row3(xa_q_norm_g), row3(xa_k_norm_g), row3(a_h_norm_g)

    mem_k, mem_v = _memkv(mem, mem_norm_g.reshape(1, D_MODEL), bf(mem_w_kv))
    xf = x.reshape(n, D_MODEL)
    for i in range(DEPTH):
        j = i // 2
        if i % 2 == 0:
            xf, xq, y_tok = _mlstm_layer(xf, g_ffn1, ffn1_in, ffn1_out, g_mix, a_main, a_gw, a_gb,
                                         g_xq, g_xk, g_h, i, j, bsz, seq)
            w_out = a_out
        else:
            xf, xq, y_tok = _swa_layer(xf, pos, b_sinks[j], g_ffn1, ffn1_in, ffn1_out, g_mix, b_main, b_gqk,
                                       rope_tab, g_xq, g_xk, i, j, bsz, seq)
            w_out = b_out
        xf = _out_ffn(xf, y_tok, xq, mem_k, mem_v, w_out, g_ffn2, ffn2_in, ffn2_out, i, j, bsz, seq)
    return xf.reshape(bsz, seq, D_MODEL)
```
